```python
import jax, jax.numpy as jnp
from jax import lax
import numpy as np

D_MODEL = 1024
BATCH = 2
SEQ = 8192
DEPTH = 2
DEC_BATCH = 128
DEC_SEQ = 8
PAST_LEN = 8192
PAGE_SIZE = 128

MLA_HEADS = 8
QK_NOPE = 64
QK_ROPE = 32
V_HEAD = 64
Q_LORA = 384
KV_LORA = 256
ROPE_THETA = 10000.0
MLA_SCALE = (QK_NOPE + QK_ROPE) ** -0.5
LRU_WIDTH = 512
LRU_BLOCKS = 8
LRU_BLOCK = LRU_WIDTH // LRU_BLOCKS
CONV_W = 4
LRU_C = 8.0
SB_HEADS = 16
SB_HEAD_DIM = D_MODEL // SB_HEADS
SB_SCALE = SB_HEAD_DIM ** -0.5
Q_BLOCK = 128
N_GROUPS = 4
EXP_PER_GROUP = 4
N_EXPERTS = N_GROUPS * EXP_PER_GROUP
TOP_K = 2
D_EXPERT = 512
ALPHA = (2.0 * DEPTH) ** 0.25
BETA = (8.0 * DEPTH) ** -0.25
LN_EPS = 1e-5
RMS_EPS = 1e-6
NEG_INF = -1e30

N_EVEN = (DEPTH + 1) // 2
N_ODD = DEPTH // 2
AB_SPLITS = (Q_LORA, Q_LORA + KV_LORA, Q_LORA + KV_LORA + QK_ROPE, Q_LORA + KV_LORA + QK_ROPE + LRU_WIDTH)
AB_IN = Q_LORA + KV_LORA + QK_ROPE + 2 * LRU_WIDTH
AB_OUT = MLA_HEADS * V_HEAD + LRU_WIDTH

kernel_name = "hybrid_mla_rglru_stickbreak_hmoe_step"


def layer_norm(x, g, b):
    xf = x.astype(jnp.float32)
    mu = jnp.mean(xf, -1, keepdims=True)
    var = jnp.mean(jnp.square(xf - mu), -1, keepdims=True)
    return ((xf - mu) * lax.rsqrt(var + LN_EPS) * g.astype(jnp.float32) + b.astype(jnp.float32)).astype(x.dtype)


def rms_norm(x, g):
    xf = x.astype(jnp.float32)
    return (xf * lax.rsqrt(jnp.mean(xf * xf, -1, keepdims=True) + RMS_EPS) * g.astype(jnp.float32)).astype(x.dtype)


def rope(x, pos):
    half = QK_ROPE // 2
    inv = ROPE_THETA ** (-jnp.arange(half, dtype=jnp.float32) / half)
    ang = pos.astype(jnp.float32)[:, None] * inv
    shape = (ang.shape[0],) + (1,) * (x.ndim - 3) + (half,)
    cos = jnp.cos(ang).reshape(shape)
    sin = jnp.sin(ang).reshape(shape)
    xf = x.astype(jnp.float32)
    x1, x2 = xf[..., :half], xf[..., half:]
    return jnp.concatenate([x1 * cos - x2 * sin, x2 * cos + x1 * sin], -1).astype(x.dtype)


def adaln(c, w_mod, b_mod):
    m = jax.nn.silu(c) @ w_mod + b_mod
    return jnp.split(m[:, None, :], 6, axis=-1)


def mla_prompt(q_nope, q_pe, ckv, kpe, w_uk, w_uv):
    B, S, H, _ = q_nope.shape
    k_nope = jnp.einsum('bsc,chn->bshn', ckv, w_uk)
    v = jnp.einsum('bsc,chv->bshv', ckv, w_uv)
    q = jnp.concatenate([q_nope, q_pe], -1)
    k = jnp.concatenate([k_nope, jnp.broadcast_to(kpe[:, :, None, :], (B, S, H, QK_ROPE))], -1)
    nb = S // Q_BLOCK
    qb = q.reshape(B, nb, Q_BLOCK, H, QK_NOPE + QK_ROPE).swapaxes(0, 1)
    kidx = jnp.arange(S)

    def block(args):
        qi, i = args
        tq = i * Q_BLOCK + jnp.arange(Q_BLOCK)
        s = jnp.einsum('bqhd,bkhd->bhqk', qi, k).astype(jnp.float32) * MLA_SCALE
        s = jnp.where(kidx[None, :] <= tq[:, None], s, NEG_INF)
        p = jax.nn.softmax(s, -1).astype(v.dtype)
        return jnp.einsum('bhqk,bkhv->bqhv', p, v)

    o = lax.map(block, (qb, jnp.arange(nb)))
    return o.swapaxes(0, 1).reshape(B, S, H, V_HEAD)


def mla_sample(q_nope, q_pe, ckv, kpe, cache_ckv, cache_kpe, e, page_table, w_uk, w_uv):
    B, T, H, _ = q_nope.shape
    past = page_table.shape[1] * PAGE_SIZE
    ckv_all = jnp.concatenate([cache_ckv[e, page_table].reshape(B, past, KV_LORA).astype(ckv.dtype), ckv], 1)
    kpe_all = jnp.concatenate([cache_kpe[e, page_table].reshape(B, past, QK_ROPE).astype(kpe.dtype), kpe], 1)
    q_abs = jnp.einsum('bthn,chn->bthc', q_nope, w_uk)
    s = (jnp.einsum('bthc,bkc->bhtk', q_abs, ckv_all)
         + jnp.einsum('bthr,bkr->bhtk', q_pe, kpe_all)).astype(jnp.float32) * MLA_SCALE
    mask = jnp.arange(past + T)[None, :] <= past + jnp.arange(T)[:, None]
    p = jax.nn.softmax(jnp.where(mask, s, NEG_INF), -1).astype(ckv.dtype)
    o_lat = jnp.einsum('bhtk,bkc->bthc', p, ckv_all)
    return jnp.einsum('bthc,chv->bthv', o_lat, w_uv)


def causal_conv(u, buf, w, b):
    T = u.shape[1]
    full = jnp.concatenate([buf.astype(u.dtype), u], 1)
    y = b
    for j in range(CONV_W):
        y = y + full[:, j:j + T] * w[j]
    return y, full[:, full.shape[1] - (CONV_W - 1):]


def rg_lru(u, h0, w_a, b_a, w_x, b_x, lam):
    B, T, W = u.shape
    ub = u.reshape(B, T, LRU_BLOCKS, LRU_BLOCK)
    r = jax.nn.sigmoid((jnp.einsum('btnk,nkj->btnj', ub, w_a).reshape(B, T, W) + b_a).astype(jnp.float32))
    i = jax.nn.sigmoid((jnp.einsum('btnk,nkj->btnj', ub, w_x).reshape(B, T, W) + b_x).astype(jnp.float32))
    log_a = -LRU_C * r * jax.nn.softplus(-lam.astype(jnp.float32))
    a = jnp.exp(log_a)
    bt = jnp.sqrt(-jnp.expm1(2.0 * log_a)) * (i * u.astype(jnp.float32))
    bt = bt.at[:, 0].add(a[:, 0] * h0.astype(jnp.float32))

    def combine(lhs, rhs):
        a_l, b_l = lhs
        a_r, b_r = rhs
        return a_l * a_r, a_r * b_l + b_r

    _, h = lax.associative_scan(combine, (a, bt), axis=1)
    return h.astype(u.dtype), h[:, -1].astype(u.dtype)


def mixer_ab(h, pos, past, e, w_in, g_q, w_uq, g_kv, w_uk, w_uv, conv_w, conv_b,
             w_ra, b_ra, w_rx, b_rx, lam, w_out):
    B, T, _ = h.shape
    q_lat, kv_lat, kpe_raw, u, gate = jnp.split(h @ w_in, AB_SPLITS, axis=-1)
    q = (rms_norm(q_lat, g_q) @ w_uq).reshape(B, T, MLA_HEADS, QK_NOPE + QK_ROPE)
    q_nope, q_pe = q[..., :QK_NOPE], rope(q[..., QK_NOPE:], pos)
    ckv = rms_norm(kv_lat, g_kv)
    kpe = rope(kpe_raw, pos)
    if past is None:
        attn = mla_prompt(q_nope, q_pe, ckv, kpe, w_uk, w_uv)
        h0 = jnp.zeros((B, LRU_WIDTH), h.dtype)
        conv_buf = jnp.zeros((B, CONV_W - 1, LRU_WIDTH), h.dtype)
    else:
        cache_ckv, cache_kpe, st_h, st_conv, _, _, page_table = past
        attn = mla_sample(q_nope, q_pe, ckv, kpe, cache_ckv, cache_kpe, e, page_table, w_uk, w_uv)
        h0 = st_h[e]
        conv_buf = st_conv[e]
    uc, conv_new = causal_conv(u, conv_buf, conv_w, conv_b)
    hseq, h_last = rg_lru(uc, h0, w_ra, b_ra, w_rx, b_rx, lam)
    lru_out = hseq * jax.nn.gelu(gate)
    out = jnp.concatenate([attn.reshape(B, T, MLA_HEADS * V_HEAD), lru_out], -1) @ w_out
    return out, (ckv, kpe, h_last, conv_new)


def sb_block(z, mask, carry):
    log_keep = jnp.where(mask, jax.nn.log_sigmoid(-z), 0.0)
    shifted = jnp.concatenate([log_keep[..., 1:], jnp.zeros_like(log_keep[..., :1])], -1)
    newer = lax.cumsum(shifted, axis=z.ndim - 1, reverse=True) + carry
    w = jnp.where(mask, jnp.exp(jax.nn.log_sigmoid(z) + newer), 0.0)
    return w, newer[..., :1] + log_keep[..., :1]


def sb_prompt(q, k, v):
    B, S, H, Dh = q.shape
    nb = S // Q_BLOCK
    qb = q.reshape(B, nb, Q_BLOCK, H, Dh).swapaxes(0, 1)
    kidx = jnp.arange(S)
    vf = v.astype(jnp.float32)
    zero = jnp.zeros((B, H, Q_BLOCK, 1), jnp.float32)

    def block(args):
        qi, i = args
        tq = i * Q_BLOCK + jnp.arange(Q_BLOCK)
        z = jnp.einsum('bqhd,bshd->bhqs', qi, k).astype(jnp.float32) * SB_SCALE
        w, _ = sb_block(z, kidx[None, :] < tq[:, None], zero)
        return jnp.einsum('bhqs,bshd->bqhd', w, vf)

    o = lax.map(block, (qb, jnp.arange(nb)))
    return o.swapaxes(0, 1).reshape(B, S, H, Dh).astype(q.dtype)


def sb_sample(q, k, v, cache_k, cache_v, o_idx, page_table):
    B, T, H, Dh = q.shape
    z = jnp.einsum('bthd,bshd->bhts', q, k).astype(jnp.float32) * SB_SCALE
    w, carry = sb_block(z, jnp.arange(T)[None, :] < jnp.arange(T)[:, None], jnp.zeros((B, H, T, 1), jnp.float32))
    acc = jnp.einsum('bhts,bshd->bthd', w, v.astype(jnp.float32))

    def page_step(state, pg):
        acc, carry = state
        kp = cache_k[o_idx, pg]
        vp = cache_v[o_idx, pg]
        zp = jnp.einsum('bthd,bshd->bhts', q, kp.astype(q.dtype)).astype(jnp.float32) * SB_SCALE
        wp, carry = sb_block(zp, True, carry)
        acc = acc + jnp.einsum('bhts,bshd->bthd', wp, vp.astype(jnp.float32))
        return (acc, carry), None

    (acc, _), _ = lax.scan(page_step, (acc, carry), page_table.T, reverse=True)
    return acc.astype(q.dtype)


def mixer_c(h, past, o_idx, w_qkv, w_out):
    B, T, _ = h.shape
    qkv = (h @ w_qkv).reshape(B, T, 3, SB_HEADS, SB_HEAD_DIM)
    q, k, v = qkv[:, :, 0], qkv[:, :, 1], qkv[:, :, 2]
    if past is None:
        o = sb_prompt(q, k, v)
    else:
        o = sb_sample(q, k, v, past[4], past[5], o_idx, past[6])
    return o.reshape(B, T, D_MODEL) @ w_out, (k, v)


def hier_moe(h, w_rg, b_rg, w_re, b_re, w1, w3, w2):
    B, T, D = h.shape
    x = h.reshape(B * T, D)
    g_logits = (x @ w_rg + b_rg).astype(jnp.float32)
    g_idx = jnp.argmax(g_logits, -1)
    g_w = jnp.take_along_axis(jax.nn.softmax(g_logits, -1), g_idx[:, None], 1)
    e_logits = (x @ w_re + b_re).astype(jnp.float32).reshape(-1, N_GROUPS, EXP_PER_GROUP)
    e_logits = jnp.take_along_axis(e_logits, g_idx[:, None, None], 1)[:, 0]
    top_p, top_i = lax.top_k(jax.nn.softmax(e_logits, -1), TOP_K)
    top_w = g_w * top_p / jnp.sum(top_p, -1, keepdims=True)
    expert_id = g_idx[:, None] * EXP_PER_GROUP + top_i
    comb = jnp.sum(jax.nn.one_hot(expert_id, N_EXPERTS, dtype=jnp.float32) * top_w[..., None], 1).astype(h.dtype)
    hid = jax.nn.silu(jnp.einsum('nd,edf->nef', x, w1)) * jnp.einsum('nd,edf->nef', x, w3)
    y = jnp.einsum('nef,efd->nd', hid * comb[:, :, None], w2)
    return y.reshape(B, T, D)


def run_group(x, c, pos, past, weights):
    (w_mod, b_mod, ln_g, ln_b, w_in_ab, g_q, w_uq, g_kv, w_uk, w_uv, conv_w, conv_b,
     w_rg_a, b_rg_a, w_rg_x, b_rg_x, lru_lambda, w_out_ab, w_qkv_c, w_out_c,
     w_router_g, b_router_g, w_router_e, b_router_e, w_e1, w_e3, w_e2) = weights
    ckv_l, kpe_l, h_l, conv_l, k_l, v_l = [], [], [], [], [], []
    for l in range(DEPTH):
        sh1, sc1, gt1, sh2, sc2, gt2 = adaln(c, w_mod[l], b_mod[l])
        h = x * (1.0 + sc1) + sh1
        if l % 2 == 0:
            e = l // 2
            mix, (ckv, kpe, h_last, conv_last) = mixer_ab(
                h, pos, past, e, w_in_ab[e], g_q[e], w_uq[e], g_kv[e], w_uk[e], w_uv[e],
                conv_w[e], conv_b[e], w_rg_a[e], b_rg_a[e], w_rg_x[e], b_rg_x[e], lru_lambda[e], w_out_ab[e])
            ckv_l.append(ckv)
            kpe_l.append(kpe)
            h_l.append(h_last)
            conv_l.append(conv_last)
        else:
            o = l // 2
            mix, (k, v) = mixer_c(h, past, o, w_qkv_c[o], w_out_c[o])
            k_l.append(k)
            v_l.append(v)
        x = layer_norm(ALPHA * x + (1.0 + gt1) * mix, ln_g[l, 0], ln_b[l, 0])
        h = x * (1.0 + sc2) + sh2
        ffn = hier_moe(h, w_router_g[l], b_router_g[l], w_router_e[l], b_router_e[l], w_e1[l], w_e3[l], w_e2[l])
        x = layer_norm(ALPHA * x + (1.0 + gt2) * ffn, ln_g[l, 1], ln_b[l, 1])
    return x, (jnp.stack(ckv_l), jnp.stack(kpe_l), jnp.stack(h_l), jnp.stack(conv_l), jnp.stack(k_l), jnp.stack(v_l))


def setup_inputs(seed: int = 0) -> dict:
    key = jax.random.key(seed)
    ks = list(jax.random.split(key, 48))

    def nrm(shape, s):
        return jax.random.normal(ks.pop(), shape, jnp.float32) * s

    n_pages = PAST_LEN // PAGE_SIZE
    n_used = DEC_BATCH * n_pages
    n_pool = n_used + (n_used + 3) // 4
    page_table = jax.random.permutation(ks.pop(), n_pool)[:n_used].astype(jnp.int32).reshape(DEC_BATCH, n_pages)
    u = jax.random.uniform(ks.pop(), (N_EVEN, LRU_WIDTH), jnp.float32, 0.9, 0.999)
    a = u ** (1.0 / LRU_C)
    lru_lambda = jnp.log(a) - jnp.log1p(-a)
    qkv_scale = jnp.array([1.0, 1.0, BETA], jnp.float32)[:, None]
    w_qkv_c = (nrm((N_ODD, D_MODEL, 3, D_MODEL), D_MODEL ** -0.5) * qkv_scale).reshape(N_ODD, D_MODEL, 3 * D_MODEL)
    d = D_MODEL
    return {
        "x_prompt": nrm((BATCH, SEQ, d), 1.0),
        "x_sample": nrm((DEC_BATCH, DEC_SEQ, d), 1.0),
        "cache_mla_ckv": nrm((N_EVEN, n_pool, PAGE_SIZE, KV_LORA), 1.0),
        "cache_mla_kpe": nrm((N_EVEN, n_pool, PAGE_SIZE, QK_ROPE), 1.0),
        "cache_sb_k": nrm((N_ODD, n_pool, PAGE_SIZE, SB_HEADS, SB_HEAD_DIM), 1.0),
        "cache_sb_v": nrm((N_ODD, n_pool, PAGE_SIZE, SB_HEADS, SB_HEAD_DIM), 0.5),
        "state_lru_h": nrm((N_EVEN, DEC_BATCH, LRU_WIDTH), 0.5),
        "state_conv": nrm((N_EVEN, DEC_BATCH, CONV_W - 1, LRU_WIDTH), 1.0),
        "page_table": page_table,
        "c_prompt": nrm((BATCH, d), 1.0),
        "c_sample": nrm((DEC_BATCH, d), 1.0),
        "w_mod": nrm((DEPTH, d, 6 * d), 0.5 * d ** -0.5),
        "b_mod": nrm((DEPTH, 6 * d), 0.01),
        "ln_g": 1.0 + nrm((DEPTH, 2, d), 0.02),
        "ln_b": nrm((DEPTH, 2, d), 0.02),
        "w_in_ab": nrm((N_EVEN, d, AB_IN), d ** -0.5),
        "g_q": 1.0 + nrm((N_EVEN, Q_LORA), 0.02),
        "w_uq": nrm((N_EVEN, Q_LORA, MLA_HEADS * (QK_NOPE + QK_ROPE)), Q_LORA ** -0.5),
        "g_kv": 1.0 + nrm((N_EVEN, KV_LORA), 0.02),
        "w_uk": nrm((N_EVEN, KV_LORA, MLA_HEADS, QK_NOPE), KV_LORA ** -0.5),
        "w_uv": nrm((N_EVEN, KV_LORA, MLA_HEADS, V_HEAD), BETA * KV_LORA ** -0.5),
        "conv_w": nrm((N_EVEN, CONV_W, LRU_WIDTH), CONV_W ** -0.5),
        "conv_b": nrm((N_EVEN, LRU_WIDTH), 0.01),
        "w_rg_a": nrm((N_EVEN, LRU_BLOCKS, LRU_BLOCK, LRU_BLOCK), LRU_BLOCK ** -0.5),
        "b_rg_a": nrm((N_EVEN, LRU_WIDTH), 0.01),
        "w_rg_x": nrm((N_EVEN, LRU_BLOCKS, LRU_BLOCK, LRU_BLOCK), LRU_BLOCK ** -0.5),
        "b_rg_x": nrm((N_EVEN, LRU_WIDTH), 0.01),
        "lru_lambda": lru_lambda,
        "w_out_ab": nrm((N_EVEN, AB_OUT, d), BETA * AB_OUT ** -0.5),
        "w_qkv_c": w_qkv_c,
        "w_out_c": nrm((N_ODD, d, d), BETA * d ** -0.5),
        "w_router_g": nrm((DEPTH, d, N_GROUPS), d ** -0.5),
        "b_router_g": nrm((DEPTH, N_GROUPS), 0.01),
        "w_router_e": nrm((DEPTH, d, N_EXPERTS), d ** -0.5),
        "b_router_e": nrm((DEPTH, N_EXPERTS), 0.01),
        "w_e1": nrm((DEPTH, N_EXPERTS, d, D_EXPERT), d ** -0.5),
        "w_e3": nrm((DEPTH, N_EXPERTS, d, D_EXPERT), d ** -0.5),
        "w_e2": nrm((DEPTH, N_EXPERTS, D_EXPERT, d), BETA * D_EXPERT ** -0.5),
    }


def reference(x_prompt, x_sample, cache_mla_ckv, cache_mla_kpe, cache_sb_k, cache_sb_v,
              state_lru_h, state_conv, page_table, c_prompt, c_sample,
              w_mod, b_mod, ln_g, ln_b, w_in_ab, g_q, w_uq, g_kv, w_uk, w_uv,
              conv_w, conv_b, w_rg_a, b_rg_a, w_rg_x, b_rg_x, lru_lambda, w_out_ab,
              w_qkv_c, w_out_c, w_router_g, b_router_g, w_router_e, b_router_e,
              w_e1, w_e3, w_e2):
    weights = (w_mod, b_mod, ln_g, ln_b, w_in_ab, g_q, w_uq, g_kv, w_uk, w_uv, conv_w, conv_b,
               w_rg_a, b_rg_a, w_rg_x, b_rg_x, lru_lambda, w_out_ab, w_qkv_c, w_out_c,
               w_router_g, b_router_g, w_router_e, b_router_e, w_e1, w_e3, w_e2)
    pos_p = jnp.arange(x_prompt.shape[1], dtype=jnp.int32)
    past_len = page_table.shape[1] * PAGE_SIZE
    pos_s = past_len + jnp.arange(x_sample.shape[1], dtype=jnp.int32)
    past = (cache_mla_ckv, cache_mla_kpe, state_lru_h, state_conv, cache_sb_k, cache_sb_v, page_table)
    y_prompt, (ckv_p, kpe_p, h_p, conv_p, k_p, v_p) = run_group(x_prompt, c_prompt, pos_p, None, weights)
    y_sample, (ckv_s, kpe_s, h_s, conv_s, k_s, v_s) = run_group(x_sample, c_sample, pos_s, past, weights)
    return (y_prompt, y_sample, ckv_p, kpe_p, h_p, conv_p, k_p, v_p, ckv_s, kpe_s, h_s, conv_s, k_s, v_s)
```

```python
import functools

import numpy as np
import jax
import jax.numpy as jnp
from jax import lax
from jax.experimental import pallas as pl
from jax.experimental.pallas import tpu as pltpu

F32, BF16 = jnp.float32, jnp.bfloat16

D_MODEL = 1024
MLA_HEADS, QK_NOPE, QK_ROPE, V_HEAD = 8, 64, 32, 64
Q_LORA, KV_LORA = 384, 256
ROPE_THETA = 10000.0
MLA_SCALE = (QK_NOPE + QK_ROPE) ** -0.5
LRU_WIDTH, LRU_BLOCKS, CONV_W, LRU_C = 512, 8, 4, 8.0
SB_HEADS, SB_HEAD_DIM = 16, 64
SB_SCALE = SB_HEAD_DIM ** -0.5
N_GROUPS, EXP_PER_GROUP, N_EXPERTS, D_EXPERT = 4, 4, 16, 512
LN_EPS, RMS_EPS = 1e-5, 1e-6
NEG_INF = -1e30
PAGE_SIZE = 128

LANES = 128
SUBLANES = 8
VMEM_LIMIT_BYTES = 56 * 1024 * 1024

HEAD_LANES = LANES
ROUTER_ROWS = 128
EXPERT_ROW0 = 8


def _dot(a, b):
    return jnp.dot(a, b, preferred_element_type=F32)


def _dot_nt(a, b):
    return lax.dot_general(a, b, (((1,), (1,)), ((), ())), preferred_element_type=F32)


def _sigmoid(x):
    return 1.0 / (1.0 + jnp.exp(-x))


def _layer_norm(y, g, b):
    mu = jnp.mean(y, axis=-1, keepdims=True)
    d = y - mu
    var = jnp.mean(d * d, axis=-1, keepdims=True)
    return d * lax.rsqrt(var + LN_EPS) * g + b


def _rms_norm(x, g):
    return x * lax.rsqrt(jnp.mean(x * x, axis=-1, keepdims=True) + RMS_EPS) * g


def _shift_of(n):
    s = int(n).bit_length() - 1
    assert 1 << s == n
    return s


def _idiv(x, n):
    return lax.shift_right_logical(x, _shift_of(n))


def _imod(x, n):
    assert 1 << _shift_of(n) == n
    return x & (n - 1)


def _act_dtype(gb):
    return BF16 if gb == 1 else F32


def _params(*sem):
    return pltpu.CompilerParams(dimension_semantics=sem, vmem_limit_bytes=VMEM_LIMIT_BYTES)


def _token_tiles(groups, rows, target):
    if rows >= target:
        assert rows % target == 0
        return 1, target
    gb = min(groups, max(1, target // rows))
    assert groups % gb == 0
    return gb, rows


def _adaln_kernel(c_ref, w_ref, b_ref, o_ref):
    c = c_ref[...]
    s = (c * _sigmoid(c)).astype(BF16)
    o_ref[0] = _dot(s, w_ref[0].astype(BF16)) + b_ref[0]


def _adaln(c, w_mod, b_mod):
    depth, d, d6 = w_mod.shape
    b = c.shape[0]
    bp = -(-b // SUBLANES) * SUBLANES
    cp = jnp.pad(c, ((0, bp - b), (0, 0)))
    tn = 1536
    out = pl.pallas_call(
        _adaln_kernel,
        out_shape=jax.ShapeDtypeStruct((depth, bp, d6), F32),
        grid=(depth, d6 // tn),
        in_specs=[
            pl.BlockSpec((bp, d), lambda l, n: (0, 0)),
            pl.BlockSpec((1, d, tn), lambda l, n: (l, 0, n)),
            pl.BlockSpec((1, 1, tn), lambda l, n: (l, 0, n)),
        ],
        out_specs=pl.BlockSpec((1, bp, tn), lambda l, n: (l, 0, n)),
        compiler_params=_params("arbitrary", "arbitrary"),
        name="adaln",
    )(cp, w_mod, b_mod.reshape(depth, 1, d6))
    m = out[:, :b].reshape(depth, b, 6, 1, d)
    return [[m[l, :, i] for i in range(6)] for l in range(depth)]


def _ab_in_kernel(x_ref, sc_ref, sh_ref, win_ref, gq_ref, gkv_ref, wqa_ref, wqb_ref, ct_ref, st_ref,
                  *rest, prompt, gb, rb):
    if prompt:
        wkp_ref, wv_ref, q_ref, kf_ref, v_ref, ckv_ref, kpe_ref, u_ref, gate_ref = rest
    else:
        q_ref, ckv_ref, kpe_ref, u_ref, gate_ref = rest
    tm = gb * rb
    h = (x_ref[...] * (1.0 + sc_ref[...]) + sh_ref[...]).reshape(tm, D_MODEL).astype(BF16)
    p = _dot(h, win_ref[...])
    o_kv, o_u, o_g, o_ka, o_kb = Q_LORA, Q_LORA + KV_LORA, Q_LORA + KV_LORA + LRU_WIDTH, \
        Q_LORA + KV_LORA + 2 * LRU_WIDTH, Q_LORA + KV_LORA + 2 * LRU_WIDTH + LANES
    q_lat, kv_lat = p[:, :o_kv], p[:, o_kv:o_u]
    u_ref[...] = p[:, o_u:o_g].reshape(gb, rb, LRU_WIDTH)
    gate_ref[...] = p[:, o_g:o_ka].reshape(gb, rb, LRU_WIDTH)
    kpe_a, kpe_b = p[:, o_ka:o_kb], p[:, o_kb:o_kb + LANES]

    c = jnp.broadcast_to(ct_ref[...][None], (gb, rb, LANES)).reshape(tm, LANES)
    s = jnp.broadcast_to(st_ref[...][None], (gb, rb, LANES)).reshape(tm, LANES)

    qn = _rms_norm(q_lat, gq_ref[...]).astype(BF16)
    qa = _dot(qn, wqa_ref[...])
    qb = _dot(qn, wqb_ref[...])
    ckv = _rms_norm(kv_lat, gkv_ref[...])
    ckv_ref[...] = ckv.reshape(gb, rb, KV_LORA)
    kpe = kpe_a * c + kpe_b * s
    kpe_ref[...] = kpe[:, :QK_ROPE].reshape(gb, rb, QK_ROPE)

    for hd in range(MLA_HEADS):
        sl = slice(hd * HEAD_LANES, (hd + 1) * HEAD_LANES)
        qh = qa[:, sl] * c + qb[:, sl] * s
        if prompt:
            q_ref[0, hd] = qh.astype(BF16)
        else:
            q_ref[:, :, sl] = qh.reshape(gb, rb, HEAD_LANES)
    if prompt:
        ckv_bf = ckv.astype(BF16)
        kin = jnp.concatenate([ckv_bf, kpe.astype(BF16)], axis=-1)
        kf = _dot(kin, wkp_ref[...])
        vv = _dot(ckv_bf, wv_ref[...])
        for hd in range(MLA_HEADS):
            kf_ref[0, hd] = kf[:, hd * HEAD_LANES:(hd + 1) * HEAD_LANES].astype(BF16)
        for hp in range(MLA_HEADS // 2):
            v_ref[0, hp] = vv[:, hp * LANES:(hp + 1) * LANES].astype(BF16)


def _ab_in(x, sc, sh, wts, ctab, stab, prompt, tm_target):
    g, r, d = x.shape
    gb, rb = _token_tiles(g, r, tm_target)
    grid = (g // gb, r // rb)
    full = lambda a: pl.BlockSpec(a.shape, lambda i, j, _n=a.ndim: (0,) * _n)
    tok = lambda c: pl.BlockSpec((gb, rb, c), lambda i, j: (i, j, 0))
    mod = pl.BlockSpec((gb, 1, d), lambda i, j: (i, 0, 0))
    tab = pl.BlockSpec((rb, LANES), lambda i, j: (j, 0))
    ins = [x, sc, sh, wts["win"], wts["gq"], wts["gkv"], wts["wqa"], wts["wqb"], ctab, stab]
    in_specs = [tok(d), mod, mod, full(wts["win"]), full(wts["gq"]), full(wts["gkv"]),
                full(wts["wqa"]), full(wts["wqb"]), tab, tab]
    tail_shapes = [jax.ShapeDtypeStruct((g, r, KV_LORA), F32), jax.ShapeDtypeStruct((g, r, QK_ROPE), F32),
                   jax.ShapeDtypeStruct((g, r, LRU_WIDTH), F32), jax.ShapeDtypeStruct((g, r, LRU_WIDTH), F32)]
    tail_specs = [tok(KV_LORA), tok(QK_ROPE), tok(LRU_WIDTH), tok(LRU_WIDTH)]
    if prompt:
        assert gb == 1
        ins += [wts["wkp"], wts["wv"]]
        in_specs += [full(wts["wkp"]), full(wts["wv"])]
        hm = lambda nh: pl.BlockSpec((1, nh, rb, LANES), lambda i, j: (i, 0, j, 0))
        out_shape = [jax.ShapeDtypeStruct((g, MLA_HEADS, r, LANES), BF16),
                     jax.ShapeDtypeStruct((g, MLA_HEADS, r, LANES), BF16),
                     jax.ShapeDtypeStruct((g, MLA_HEADS // 2, r, LANES), BF16)] + tail_shapes
        out_specs = [hm(MLA_HEADS), hm(MLA_HEADS), hm(MLA_HEADS // 2)] + tail_specs
    else:
        out_shape = [jax.ShapeDtypeStruct((g, r, MLA_HEADS * HEAD_LANES), F32)] + tail_shapes
        out_specs = [tok(MLA_HEADS * HEAD_LANES)] + tail_specs
    return pl.pallas_call(
        functools.partial(_ab_in_kernel, prompt=prompt, gb=gb, rb=rb),
        out_shape=out_shape, grid=grid, in_specs=in_specs, out_specs=out_specs,
        compiler_params=_params("arbitrary", "arbitrary"),
        name="ab_in_prompt" if prompt else "ab_in_sample",
    )(*ins)


def _triangle(nq, descending):
    qi, kj = [], []
    for q in range(nq):
        ks = range(q, -1, -1) if descending else range(q + 1)
        for k in ks:
            qi.append(q)
            kj.append(k)
    return jnp.asarray(np.array(qi, np.int32)), jnp.asarray(np.array(kj, np.int32))


def _mla_prompt_kernel(qi_ref, kj_ref, q_ref, k_ref, v_ref, o_ref, m_scr, l_scr, acc_scr, *, tq):
    t = pl.program_id(1)
    qi, kj = qi_ref[t], kj_ref[t]

    @pl.when(kj == 0)
    def _init():
        m_scr[...] = jnp.full(m_scr.shape, NEG_INF, F32)
        l_scr[...] = jnp.zeros(l_scr.shape, F32)
        acc_scr[...] = jnp.zeros(acc_scr.shape, F32)

    def run(diagonal):
        if diagonal:
            row = lax.broadcasted_iota(jnp.int32, (tq, tq), 0)
            col = lax.broadcasted_iota(jnp.int32, (tq, tq), 1)
            valid = col <= row

        def body(hp, carry):
            vv = v_ref[0, hp]
            for sub in range(2):
                hd = 2 * hp + sub
                s = _dot_nt(q_ref[0, hd], k_ref[0, hd]) * MLA_SCALE
                if diagonal:
                    s = jnp.where(valid, s, NEG_INF)
                m_prev = m_scr[hd]
                m_new = jnp.maximum(m_prev, jnp.max(s, axis=-1, keepdims=True))
                alpha = jnp.exp(m_prev - m_new)
                p = jnp.exp(s - m_new)
                l_scr[hd] = alpha * l_scr[hd] + jnp.sum(p, axis=-1, keepdims=True)
                acc_scr[hd] = alpha * acc_scr[hd] + _dot(p.astype(BF16), vv)
                m_scr[hd] = m_new
            return carry

        lax.fori_loop(0, MLA_HEADS // 2, body, 0)

    @pl.when(kj < qi)
    def _off():
        run(False)

    @pl.when(kj == qi)
    def _diag():
        run(True)
        lane = lax.broadcasted_iota(jnp.int32, (tq, LANES), 1)
        for hp in range(MLA_HEADS // 2):
            o0 = acc_scr[2 * hp] / l_scr[2 * hp]
            o1 = acc_scr[2 * hp + 1] / l_scr[2 * hp + 1]
            o_ref[0, :, hp * LANES:(hp + 1) * LANES] = jnp.where(lane < V_HEAD, o0, o1).astype(BF16)


def _mla_prompt(q, k, v, tq):
    b, nh, s, _ = q.shape
    nq = s // tq
    qi, kj = _triangle(nq, descending=False)
    grid_spec = pltpu.PrefetchScalarGridSpec(
        num_scalar_prefetch=2, grid=(b, qi.shape[0]),
        in_specs=[
            pl.BlockSpec((1, nh, tq, LANES), lambda bb, t, qi_r, kj_r: (bb, 0, qi_r[t], 0)),
            pl.BlockSpec((1, nh, tq, LANES), lambda bb, t, qi_r, kj_r: (bb, 0, kj_r[t], 0)),
            pl.BlockSpec((1, nh // 2, tq, LANES), lambda bb, t, qi_r, kj_r: (bb, 0, kj_r[t], 0)),
        ],
        out_specs=pl.BlockSpec((1, tq, nh * V_HEAD), lambda bb, t, qi_r, kj_r: (bb, qi_r[t], 0)),
        scratch_shapes=[pltpu.VMEM((nh, tq, 1), F32), pltpu.VMEM((nh, tq, 1), F32),
                        pltpu.VMEM((nh, tq, LANES), F32)],
    )
    return pl.pallas_call(
        functools.partial(_mla_prompt_kernel, tq=tq),
        out_shape=jax.ShapeDtypeStruct((b, s, nh * V_HEAD), BF16),
        grid_spec=grid_spec,
        compiler_params=_params("arbitrary", "arbitrary"),
        name="mla_prompt",
    )(qi, kj, q, k, v)


def _mla_sample_kernel(pt_ref, q_ref, cn_ref, kn_ref, wuk_ref, wuv_ref, *rest, pages, rows):
    c_refs, k_refs = rest[:pages], rest[pages:2 * pages]
    o_ref, qabs_scr, qpe_scr, m_scr, l_scr, acc_scr = rest[2 * pages:]
    j = pl.program_id(1)
    nrow = MLA_HEADS * rows

    def block(ckv_bf, kpe_bf, valid):
        s = (_dot_nt(qabs_scr[...], ckv_bf) + _dot_nt(qpe_scr[...], kpe_bf)) * MLA_SCALE
        if valid is not None:
            s = jnp.where(valid, s, NEG_INF)
        m_prev = m_scr[...]
        m_new = jnp.maximum(m_prev, jnp.max(s, axis=-1, keepdims=True))
        alpha = jnp.exp(m_prev - m_new)
        p = jnp.exp(s - m_new)
        l_scr[...] = alpha * l_scr[...] + jnp.sum(p, axis=-1, keepdims=True)
        acc_scr[...] = alpha * acc_scr[...] + _dot(p.astype(BF16), ckv_bf)
        m_scr[...] = m_new

    @pl.when(j == 0)
    def _first():
        q = q_ref[0]
        q64 = jnp.concatenate([q[:, hd * HEAD_LANES:(hd + 1) * HEAD_LANES] for hd in range(MLA_HEADS)],
                              axis=0)
        q64_bf = q64.astype(BF16)
        rgrp = _idiv(lax.broadcasted_iota(jnp.int32, (nrow, KV_LORA), 0), rows)
        qabs = jnp.zeros((nrow, KV_LORA), F32)
        for hd in range(MLA_HEADS):
            qabs = qabs + jnp.where(rgrp == hd, _dot(q64_bf, wuk_ref[hd]), 0.0)
        qabs_scr[...] = qabs.astype(BF16)
        qpe_scr[...] = q64_bf[:, :QK_ROPE]
        m_scr[...] = jnp.full(m_scr.shape, NEG_INF, F32)
        l_scr[...] = jnp.zeros(l_scr.shape, F32)
        acc_scr[...] = jnp.zeros(acc_scr.shape, F32)
        pad = PAGE_SIZE - rows
        cn = jnp.concatenate([cn_ref[0], jnp.zeros((pad, KV_LORA), F32)], axis=0).astype(BF16)
        kn = jnp.concatenate([kn_ref[0], jnp.zeros((pad, QK_ROPE), F32)], axis=0).astype(BF16)
        row = lax.broadcasted_iota(jnp.int32, (nrow, PAGE_SIZE), 0)
        col = lax.broadcasted_iota(jnp.int32, (nrow, PAGE_SIZE), 1)
        block(cn, kn, col <= _imod(row, rows))

    for i in range(pages):
        block(c_refs[i][0, 0].astype(BF16), k_refs[i][0, 0].astype(BF16), None)

    @pl.when(j == pl.num_programs(1) - 1)
    def _last():
        o_lat = (acc_scr[...] / l_scr[...]).astype(BF16)
        o_full = _dot(o_lat, wuv_ref[...])
        lane_h = _idiv(lax.broadcasted_iota(jnp.int32, (rows, MLA_HEADS * V_HEAD), 1), V_HEAD)
        out = jnp.zeros((rows, MLA_HEADS * V_HEAD), F32)
        for hd in range(MLA_HEADS):
            out = out + jnp.where(lane_h == hd, o_full[hd * rows:(hd + 1) * rows], 0.0)
        o_ref[0] = out


def _mla_sample(q, ckv_new, kpe_new, cache_ckv, cache_kpe, e, page_table, wukp, wuv, pages):
    b, rows, _ = q.shape
    n_pages = page_table.shape[1]
    assert n_pages % pages == 0 and rows == SUBLANES
    steps = n_pages // pages
    pt = page_table.reshape(-1)
    nrow = MLA_HEADS * rows

    def page_spec(width, i):
        return pl.BlockSpec((1, 1, PAGE_SIZE, width),
                            lambda bb, j, pt_r, _i=i: (e, pt_r[bb * n_pages + j * pages + _i], 0, 0))

    full = lambda a: pl.BlockSpec(a.shape, lambda bb, j, pt_r, _n=a.ndim: (0,) * _n)
    seq = lambda c: pl.BlockSpec((1, rows, c), lambda bb, j, pt_r: (bb, 0, 0))
    grid_spec = pltpu.PrefetchScalarGridSpec(
        num_scalar_prefetch=1, grid=(b, steps),
        in_specs=[seq(q.shape[-1]), seq(KV_LORA), seq(QK_ROPE), full(wukp), full(wuv)]
        + [page_spec(KV_LORA, i) for i in range(pages)] + [page_spec(QK_ROPE, i) for i in range(pages)],
        out_specs=seq(MLA_HEADS * V_HEAD),
        scratch_shapes=[pltpu.VMEM((nrow, KV_LORA), BF16), pltpu.VMEM((nrow, QK_ROPE), BF16),
                        pltpu.VMEM((nrow, 1), F32), pltpu.VMEM((nrow, 1), F32),
                        pltpu.VMEM((nrow, KV_LORA), F32)],
    )
    return pl.pallas_call(
        functools.partial(_mla_sample_kernel, pages=pages, rows=rows),
        out_shape=jax.ShapeDtypeStruct((b, rows, MLA_HEADS * V_HEAD), F32),
        grid_spec=grid_spec,
        compiler_params=_params("arbitrary", "arbitrary"),
        name="mla_sample",
    )(pt, q, ckv_new, kpe_new, wukp, wuv, *([cache_ckv] * pages), *([cache_kpe] * pages))


def _gelu_tanh(x):
    return x * (0.5 * (1.0 + jnp.tanh(0.7978845608028654 * (x + 0.044715 * (x * x * x)))))


def _lru_kernel(u_ref, gate_ref, cbuf_ref, h0_ref, cw_ref, cb_ref, wa_ref, ba_ref, wx_ref, bx_ref, lam_ref,
                out_ref, hlast_ref, cnew_ref, ext_scr, a_scr, b_scr, h_scr, hc_scr, *, gb, rb):
    j = pl.program_id(1)
    tm, w = gb * rb, LRU_WIDTH
    pre = SUBLANES
    tail = CONV_W - 1

    @pl.when(j == 0)
    def _first():
        ext_scr[:, 0:pre, :] = jnp.zeros((gb, pre, w), F32)
        ext_scr[:, pre - tail:pre, :] = cbuf_ref[...]
        hc_scr[...] = h0_ref[...]

    ext_scr[:, pre:pre + rb, :] = u_ref[...]
    uc = cb_ref[...][None]
    for t in range(CONV_W):
        uc = uc + ext_scr[:, pre - tail + t:pre - tail + t + rb, :] * cw_ref[t:t + 1, :][None]
    new_tail = ext_scr[:, pre + rb - tail:pre + rb, :]
    ext_scr[:, pre - tail:pre, :] = new_tail
    cnew_ref[...] = new_tail

    uc = uc.reshape(tm, w)
    ub = uc.astype(BF16)
    r = _sigmoid(_dot(ub, wa_ref[...]) + ba_ref[...])
    i = _sigmoid(_dot(ub, wx_ref[...]) + bx_ref[...])
    nl = -lam_ref[...]
    softplus = jnp.maximum(nl, 0.0) + jnp.log(1.0 + jnp.exp(-jnp.abs(nl)))
    log_a = (-LRU_C) * r * softplus
    a = jnp.exp(log_a)
    bt = jnp.sqrt(1.0 - jnp.exp(2.0 * log_a)) * (i * uc)

    rin = _imod(lax.broadcasted_iota(jnp.int32, (tm, w), 0), SUBLANES)
    for sft in (1, 2, 4):
        a_sh = pltpu.roll(a, sft, 0)
        b_sh = pltpu.roll(bt, sft, 0)
        m = rin >= sft
        bt = jnp.where(m, a * b_sh + bt, bt)
        a = jnp.where(m, a * a_sh, a)

    if rb == SUBLANES:
        hseq = a.reshape(gb, rb, w) * hc_scr[...] + bt.reshape(gb, rb, w)
        hc_scr[...] = hseq[:, rb - 1:rb, :]
    else:
        a_scr[...] = a
        b_scr[...] = bt

        def body(g, hprev):
            r0 = pl.multiple_of(g * SUBLANES, SUBLANES)
            hh = a_scr[pl.ds(r0, SUBLANES), :] * hprev + b_scr[pl.ds(r0, SUBLANES), :]
            h_scr[pl.ds(r0, SUBLANES), :] = hh
            return hh[SUBLANES - 1:SUBLANES, :]

        hc_scr[0] = lax.fori_loop(0, rb // SUBLANES, body, hc_scr[0])
        hseq = h_scr[...].reshape(gb, rb, w)
    hlast_ref[...] = hc_scr[...]
    out_ref[...] = (hseq * _gelu_tanh(gate_ref[...])).astype(out_ref.dtype)


def _lru(u, gate, conv_buf, h0, wts, tm_target):
    g, r, w = u.shape
    gb, rb = _token_tiles(g, r, tm_target)
    assert gb == 1 or rb == SUBLANES
    tm = gb * rb
    full = lambda a: pl.BlockSpec(a.shape, lambda i, j, _n=a.ndim: (0,) * _n)
    tok = pl.BlockSpec((gb, rb, w), lambda i, j: (i, j, 0))
    per_g = lambda rows: pl.BlockSpec((gb, rows, w), lambda i, j: (i, 0, 0))
    names = ["conv_w", "conv_b", "wa", "ba", "wx", "bx", "lam"]
    return pl.pallas_call(
        functools.partial(_lru_kernel, gb=gb, rb=rb),
        out_shape=[jax.ShapeDtypeStruct((g, r, w), _act_dtype(gb)), jax.ShapeDtypeStruct((g, 1, w), F32),
                   jax.ShapeDtypeStruct((g, CONV_W - 1, w), F32)],
        grid=(g // gb, r // rb),
        in_specs=[tok, tok, per_g(CONV_W - 1), per_g(1)] + [full(wts[n]) for n in names],
        out_specs=[tok, per_g(1), per_g(CONV_W - 1)],
        scratch_shapes=[pltpu.VMEM((gb, SUBLANES + rb, w), F32), pltpu.VMEM((tm, w), F32),
                        pltpu.VMEM((tm, w), F32), pltpu.VMEM((tm, w), F32), pltpu.VMEM((gb, 1, w), F32)],
        compiler_params=_params("arbitrary", "arbitrary"),
        name="lru",
    )(u, gate, conv_buf, h0, *[wts[n] for n in names])


def _mix_out_kernel(*refs, n_in, gb, rb, alpha):
    a_refs, w_refs = refs[:n_in], refs[n_in:2 * n_in]
    x_ref, gt_ref, lng_ref, lnb_ref, sc2_ref, sh2_ref, wr_ref, br_ref, x1_ref, h2_ref, lg_ref = refs[2 * n_in:]
    tm = gb * rb
    mix = None
    for a_ref, w_ref in zip(a_refs, w_refs):
        part = _dot(a_ref[...].reshape(tm, a_ref.shape[-1]).astype(BF16), w_ref[...])
        mix = part if mix is None else mix + part
    y = alpha * x_ref[...] + (1.0 + gt_ref[...]) * mix.reshape(gb, rb, D_MODEL)
    x1 = _layer_norm(y, lng_ref[...], lnb_ref[...])
    x1_ref[...] = x1
    h2 = x1 * (1.0 + sc2_ref[...]) + sh2_ref[...]
    h2_ref[...] = h2.astype(h2_ref.dtype)
    lg_ref[...] = _dot_nt(wr_ref[...], h2.reshape(tm, D_MODEL).astype(BF16)) + br_ref[...]


def _mix_out(acts, ws, x, gt, lng, lnb, sc2, sh2, wr, br, alpha, tm_target):
    g, r, d = x.shape
    gb, rb = _token_tiles(g, r, tm_target)
    tm = gb * rb
    nj = r // rb
    full = lambda a: pl.BlockSpec(a.shape, lambda i, j, _n=a.ndim: (0,) * _n)
    tok = lambda c: pl.BlockSpec((gb, rb, c), lambda i, j: (i, j, 0))
    mod = pl.BlockSpec((gb, 1, d), lambda i, j: (i, 0, 0))
    return pl.pallas_call(
        functools.partial(_mix_out_kernel, n_in=len(acts), gb=gb, rb=rb, alpha=alpha),
        out_shape=[jax.ShapeDtypeStruct((g, r, d), F32), jax.ShapeDtypeStruct((g, r, d), _act_dtype(gb)),
                   jax.ShapeDtypeStruct((ROUTER_ROWS, g * r), F32)],
        grid=(g // gb, nj),
        in_specs=[tok(a.shape[-1]) for a in acts] + [full(w) for w in ws]
        + [tok(d), mod, full(lng), full(lnb), mod, mod, full(wr), full(br)],
        out_specs=[tok(d), tok(d), pl.BlockSpec((ROUTER_ROWS, tm), lambda i, j: (0, i * nj + j))],
        compiler_params=_params("arbitrary", "arbitrary"),
        name="mix_out",
    )(*acts, *ws, x, gt, lng, lnb, sc2, sh2, wr, br)


def _route_kernel(lg_ref, comb_ref):
    tn = lg_ref.shape[1]
    big = 3.0e38
    grow = lax.broadcasted_iota(jnp.int32, (SUBLANES, tn), 0)
    gl = jnp.where(grow < N_GROUPS, lg_ref[0:SUBLANES, :], -big)
    gmax = jnp.max(gl, axis=0, keepdims=True)
    g_idx = jnp.min(jnp.where(gl == gmax, grow, N_GROUPS), axis=0, keepdims=True)
    g_w = 1.0 / jnp.sum(jnp.where(grow < N_GROUPS, jnp.exp(gl - gmax), 0.0), axis=0, keepdims=True)

    el = lg_ref[EXPERT_ROW0:EXPERT_ROW0 + N_EXPERTS, :]
    erow = lax.broadcasted_iota(jnp.int32, (N_EXPERTS, tn), 0)
    ingrp = _idiv(erow, EXP_PER_GROUP) == g_idx
    emax = jnp.max(jnp.where(ingrp, el, -big), axis=0, keepdims=True)
    ex = jnp.where(ingrp, jnp.exp(el - emax), 0.0)
    p = ex / jnp.sum(ex, axis=0, keepdims=True)
    ps = jnp.where(ingrp, p, -1.0)
    p1 = jnp.max(ps, axis=0, keepdims=True)
    i1 = jnp.min(jnp.where(ps == p1, erow, N_EXPERTS), axis=0, keepdims=True)
    ps2 = jnp.where(erow == i1, -1.0, ps)
    p2 = jnp.max(ps2, axis=0, keepdims=True)
    i2 = jnp.min(jnp.where(ps2 == p2, erow, N_EXPERTS), axis=0, keepdims=True)
    tot = p1 + p2
    comb_ref[...] = (jnp.where(erow == i1, g_w * p1 / tot, 0.0)
                     + jnp.where(erow == i2, g_w * p2 / tot, 0.0))


def _route(logits_t):
    n = logits_t.shape[1]
    tn = min(n, 2048)
    return pl.pallas_call(
        _route_kernel,
        out_shape=jax.ShapeDtypeStruct((N_EXPERTS, n), F32),
        grid=(n // tn,),
        in_specs=[pl.BlockSpec((ROUTER_ROWS, tn), lambda i: (0, i))],
        out_specs=pl.BlockSpec((N_EXPERTS, tn), lambda i: (0, i)),
        compiler_params=_params("arbitrary"),
        name="route",
    )(logits_t)


def _moe_kernel(h_ref, comb_ref, w1_ref, w3_ref, w2_ref, x1_ref, gt_ref, lng_ref, lnb_ref, out_ref, acc_scr,
                *, gb, rb, alpha):
    e = pl.program_id(2)
    tm = gb * rb

    @pl.when(e == 0)
    def _init():
        acc_scr[...] = jnp.zeros(acc_scr.shape, F32)

    h = h_ref[...].reshape(tm, D_MODEL).astype(BF16)
    a = _dot(h, w1_ref[0])
    b = _dot(h, w3_ref[0])
    comb = comb_ref[...].reshape(tm, N_EXPERTS)
    lane = lax.broadcasted_iota(jnp.int32, (tm, N_EXPERTS), 1)
    ce = jnp.sum(jnp.where(lane == e, comb, 0.0), axis=1, keepdims=True)
    hid = (a * _sigmoid(a)) * b * ce
    acc_scr[...] += _dot(hid.astype(BF16), w2_ref[0])

    @pl.when(e == pl.num_programs(2) - 1)
    def _fin():
        y = alpha * x1_ref[...] + (1.0 + gt_ref[...]) * acc_scr[...].reshape(gb, rb, D_MODEL)
        out_ref[...] = _layer_norm(y, lng_ref[...], lnb_ref[...])


def _moe(h2, comb, w1, w3, w2, x1, gt, lng, lnb, alpha, tm_target):
    g, r, d = x1.shape
    gb, rb = _token_tiles(g, r, tm_target)
    ne = w1.shape[0]
    tok = lambda c: pl.BlockSpec((gb, rb, c), lambda i, j, e: (i, j, 0))
    full = lambda a: pl.BlockSpec(a.shape, lambda i, j, e, _n=a.ndim: (0,) * _n)
    wspec = lambda a: pl.BlockSpec((1,) + a.shape[1:], lambda i, j, e: (e, 0, 0))
    return pl.pallas_call(
        functools.partial(_moe_kernel, gb=gb, rb=rb, alpha=alpha),
        out_shape=jax.ShapeDtypeStruct((g, r, d), F32),
        grid=(g // gb, r // rb, ne),
        in_specs=[tok(d), tok(N_EXPERTS), wspec(w1), wspec(w3), wspec(w2), tok(d),
                  pl.BlockSpec((gb, 1, d), lambda i, j, e: (i, 0, 0)), full(lng), full(lnb)],
        out_specs=tok(d),
        scratch_shapes=[pltpu.VMEM((gb * rb, d), F32)],
        compiler_params=_params("arbitrary", "arbitrary", "arbitrary"),
        name="moe",
    )(h2, comb, w1, w3, w2, x1, gt, lng, lnb)


def _sb_qkv_kernel(x_ref, sc_ref, sh_ref, w_ref, *outs, prompt, gb, rb):
    tm = gb * rb
    h = (x_ref[...] * (1.0 + sc_ref[...]) + sh_ref[...]).reshape(tm, D_MODEL).astype(BF16)
    qkv = _dot(h, w_ref[...])
    q = qkv[:, :D_MODEL] * SB_SCALE
    k = qkv[:, D_MODEL:2 * D_MODEL]
    v = qkv[:, 2 * D_MODEL:]
    if prompt:
        k_ref, v_ref, qh_ref, kh_ref, vh_ref = outs
        for hp in range(SB_HEADS // 2):
            sl = slice(hp * LANES, (hp + 1) * LANES)
            qh_ref[0, hp] = q[:, sl].astype(BF16)
            kh_ref[0, hp] = k[:, sl].astype(BF16)
            vh_ref[0, hp] = v[:, sl].astype(BF16)
    else:
        k_ref, v_ref, q_ref = outs
        q_ref[...] = q.reshape(gb, rb, D_MODEL)
    k_ref[...] = k.reshape(gb, rb, D_MODEL)
    v_ref[...] = v.reshape(gb, rb, D_MODEL)


def _sb_qkv(x, sc, sh, w, prompt, tm_target):
    g, r, d = x.shape
    gb, rb = _token_tiles(g, r, tm_target)
    tok = pl.BlockSpec((gb, rb, d), lambda i, j: (i, j, 0))
    mod = pl.BlockSpec((gb, 1, d), lambda i, j: (i, 0, 0))
    out_shape = [jax.ShapeDtypeStruct((g, r, d), F32)] * 2
    out_specs = [tok, tok]
    if prompt:
        assert gb == 1
        npair = SB_HEADS // 2
        out_shape += [jax.ShapeDtypeStruct((g, npair, r, LANES), BF16)] * 3
        out_specs += [pl.BlockSpec((1, npair, rb, LANES), lambda i, j: (i, 0, j, 0))] * 3
    else:
        out_shape += [jax.ShapeDtypeStruct((g, r, d), F32)]
        out_specs += [tok]
    return pl.pallas_call(
        functools.partial(_sb_qkv_kernel, prompt=prompt, gb=gb, rb=rb),
        out_shape=out_shape, grid=(g // gb, r // rb),
        in_specs=[tok, mod, mod, pl.BlockSpec(w.shape, lambda i, j: (0, 0))],
        out_specs=out_specs,
        compiler_params=_params("arbitrary", "arbitrary"),
        name="sb_qkv_prompt" if prompt else "sb_qkv_sample",
    )(x, sc, sh, w)


def _sb_block(z, upper, carry, valid):
    l1p = jnp.log(1.0 + jnp.exp(-jnp.abs(z)))
    lk = -(jnp.maximum(z, 0.0) + l1p)
    lq = jnp.minimum(z, 0.0) - l1p
    if valid is not None:
        lk = jnp.where(valid, lk, 0.0)
    hi = lk.astype(BF16)
    lo = (lk - hi.astype(F32)).astype(BF16)
    newer = _dot(hi, upper) + _dot(lo, upper) + carry
    w = jnp.exp(lq + newer)
    if valid is not None:
        w = jnp.where(valid, w, 0.0)
    return w, carry + jnp.sum(lk, axis=-1, keepdims=True)


def _sb_prompt_kernel(qi_ref, kj_ref, q_ref, k_ref, v_ref, up_ref, o_ref, carry_scr, acc_scr, *, tq):
    t = pl.program_id(1)
    qi, kj = qi_ref[t], kj_ref[t]
    npair = SB_HEADS // 2

    @pl.when(kj == qi)
    def _init():
        carry_scr[...] = jnp.zeros(carry_scr.shape, F32)
        acc_scr[...] = jnp.zeros(acc_scr.shape, F32)

    def run(diagonal):
        valid = None
        if diagonal:
            row = lax.broadcasted_iota(jnp.int32, (tq, tq), 0)
            col = lax.broadcasted_iota(jnp.int32, (tq, tq), 1)
            valid = col < row
        lane = lax.broadcasted_iota(jnp.int32, (tq, LANES), 1)
        upper = up_ref[...]

        def body(hp, c):
            qq, kk, vv = q_ref[0, hp].astype(F32), k_ref[0, hp], v_ref[0, hp]
            for sub in range(2):
                hsel = (lane >= SB_HEAD_DIM) if sub else (lane < SB_HEAD_DIM)
                z = _dot_nt(jnp.where(hsel, qq, 0.0).astype(BF16), kk)
                w, carry = _sb_block(z, upper, carry_scr[hp, sub], valid)
                carry_scr[hp, sub] = carry
                acc_scr[hp] += jnp.where(hsel, _dot(w.astype(BF16), vv), 0.0)
            return c

        lax.fori_loop(0, npair, body, 0)

    @pl.when(kj == qi)
    def _diag():
        run(True)

    @pl.when(kj < qi)
    def _off():
        run(False)

    @pl.when(kj == 0)
    def _fin():
        for hp in range(npair):
            o_ref[0, :, hp * LANES:(hp + 1) * LANES] = acc_scr[hp].astype(BF16)


def _upper(n):
    j = np.arange(n)[:, None]
    s = np.arange(n)[None, :]
    return jnp.asarray((j > s).astype(np.float32), dtype=BF16)


def _sb_prompt(q, k, v, tq):
    b, npair, s, _ = q.shape
    nq = s // tq
    qi, kj = _triangle(nq, descending=True)
    up = _upper(tq)
    blk = lambda sel: pl.BlockSpec((1, npair, tq, LANES), sel)
    grid_spec = pltpu.PrefetchScalarGridSpec(
        num_scalar_prefetch=2, grid=(b, qi.shape[0]),
        in_specs=[blk(lambda bb, t, qi_r, kj_r: (bb, 0, qi_r[t], 0)),
                  blk(lambda bb, t, qi_r, kj_r: (bb, 0, kj_r[t], 0)),
                  blk(lambda bb, t, qi_r, kj_r: (bb, 0, kj_r[t], 0)),
                  pl.BlockSpec((tq, tq), lambda bb, t, qi_r, kj_r: (0, 0))],
        out_specs=pl.BlockSpec((1, tq, npair * LANES), lambda bb, t, qi_r, kj_r: (bb, qi_r[t], 0)),
        scratch_shapes=[pltpu.VMEM((npair, 2, tq, 1), F32), pltpu.VMEM((npair, tq, LANES), F32)],
    )
    return pl.pallas_call(
        functools.partial(_sb_prompt_kernel, tq=tq),
        out_shape=jax.ShapeDtypeStruct((b, s, npair * LANES), BF16),
        grid_spec=grid_spec,
        compiler_params=_params("arbitrary", "arbitrary"),
        name="sb_prompt",
    )(qi, kj, q, k, v, up)


def _sb_sample_kernel(pt_ref, q_ref, kn_ref, vn_ref, up_ref, *rest, pages, rows):
    k_refs, v_refs = rest[:pages], rest[pages:2 * pages]
    o_ref, qbd_scr, carry_scr, acc_scr = rest[2 * pages:]
    j = pl.program_id(1)
    nrow = SB_HEADS * rows
    upper = up_ref[...]

    def block(kk, vv, valid):
        z = _dot_nt(qbd_scr[...], kk)
        w, carry = _sb_block(z, upper, carry_scr[...], valid)
        carry_scr[...] = carry
        acc_scr[...] += _dot(w.astype(BF16), vv)

    @pl.when(j == 0)
    def _first():
        qt = jnp.concatenate([q_ref[0]] * SB_HEADS, axis=0)
        rgrp = _idiv(lax.broadcasted_iota(jnp.int32, (nrow, D_MODEL), 0), rows)
        lgrp = _idiv(lax.broadcasted_iota(jnp.int32, (nrow, D_MODEL), 1), SB_HEAD_DIM)
        qbd_scr[...] = jnp.where(rgrp == lgrp, qt, 0.0).astype(BF16)
        carry_scr[...] = jnp.zeros(carry_scr.shape, F32)
        acc_scr[...] = jnp.zeros(acc_scr.shape, F32)
        pad = jnp.zeros((PAGE_SIZE - rows, D_MODEL), F32)
        kn = jnp.concatenate([kn_ref[0], pad], axis=0).astype(BF16)
        vn = jnp.concatenate([vn_ref[0], pad], axis=0).astype(BF16)
        row = lax.broadcasted_iota(jnp.int32, (nrow, PAGE_SIZE), 0)
        col = lax.broadcasted_iota(jnp.int32, (nrow, PAGE_SIZE), 1)
        block(kn, vn, col < _imod(row, rows))

    for i in range(pages):
        block(k_refs[i][0, 0].astype(BF16), v_refs[i][0, 0].astype(BF16), None)

    @pl.when(j == pl.num_programs(1) - 1)
    def _last():
        lgrp = _idiv(lax.broadcasted_iota(jnp.int32, (rows, D_MODEL), 1), SB_HEAD_DIM)
        out = jnp.zeros((rows, D_MODEL), F32)
        for hd in range(SB_HEADS):
            out = out + jnp.where(lgrp == hd, acc_scr[hd * rows:(hd + 1) * rows, :], 0.0)
        o_ref[0] = out


def _sb_sample(q, k_new, v_new, cache_k, cache_v, o_idx, page_table, pages):
    b, rows, d = q.shape
    n_pages = page_table.shape[1]
    assert n_pages % pages == 0 and rows == SUBLANES
    steps = n_pages // pages
    pt = page_table.reshape(-1)
    nrow = SB_HEADS * rows
    up = _upper(PAGE_SIZE)

    def page_spec(i):
        return pl.BlockSpec(
            (1, 1, PAGE_SIZE, d),
            lambda bb, j, pt_r, _i=i: (o_idx, pt_r[bb * n_pages + n_pages - 1 - (j * pages + _i)], 0, 0))

    seq = pl.BlockSpec((1, rows, d), lambda bb, j, pt_r: (bb, 0, 0))
    grid_spec = pltpu.PrefetchScalarGridSpec(
        num_scalar_prefetch=1, grid=(b, steps),
        in_specs=[seq, seq, seq, pl.BlockSpec(up.shape, lambda bb, j, pt_r: (0, 0))]
        + [page_spec(i) for i in range(pages)] * 2,
        out_specs=seq,
        scratch_shapes=[pltpu.VMEM((nrow, d), BF16), pltpu.VMEM((nrow, 1), F32), pltpu.VMEM((nrow, d), F32)],
    )
    return pl.pallas_call(
        functools.partial(_sb_sample_kernel, pages=pages, rows=rows),
        out_shape=jax.ShapeDtypeStruct((b, rows, d), F32),
        grid_spec=grid_spec,
        compiler_params=_params("arbitrary", "arbitrary"),
        name="sb_sample",
    )(pt, q, k_new, v_new, up, *([cache_k] * pages), *([cache_v] * pages))


def _prep_layer0(e, w_in_ab, g_q, w_uq, g_kv, w_uk, w_uv, conv_w, conv_b, w_rg_a, b_rg_a, w_rg_x, b_rg_x,
                 lru_lambda):
    half = QK_ROPE // 2
    w_in = w_in_ab[e]
    o1, o2, o3, o4 = Q_LORA, Q_LORA + KV_LORA, Q_LORA + KV_LORA + QK_ROPE, Q_LORA + KV_LORA + QK_ROPE + LRU_WIDTH
    kpe_w = w_in[:, o2:o3]
    zpad = jnp.zeros((D_MODEL, LANES - QK_ROPE), F32)
    win = jnp.concatenate([
        w_in[:, :o2], w_in[:, o3:o4], w_in[:, o4:],
        kpe_w, zpad,
        kpe_w[:, half:], kpe_w[:, :half], zpad], axis=1).astype(BF16)

    wq = w_uq[e].reshape(Q_LORA, MLA_HEADS, QK_NOPE + QK_ROPE)
    nope, x1, x2 = wq[..., :QK_NOPE], wq[..., QK_NOPE:QK_NOPE + half], wq[..., QK_NOPE + half:]
    z32 = jnp.zeros((Q_LORA, MLA_HEADS, HEAD_LANES - QK_NOPE - QK_ROPE), F32)
    wqa = jnp.concatenate([x1, x2, z32, nope], axis=-1).reshape(Q_LORA, MLA_HEADS * HEAD_LANES).astype(BF16)
    wqb = jnp.concatenate([x2, x1, z32, jnp.zeros_like(nope)], axis=-1)
    wqb = wqb.reshape(Q_LORA, MLA_HEADS * HEAD_LANES).astype(BF16)

    wk_top = jnp.concatenate([jnp.zeros((KV_LORA, MLA_HEADS, HEAD_LANES - QK_NOPE), F32), w_uk[e]], axis=-1)
    eye = jnp.eye(LANES, HEAD_LANES, dtype=F32) * (jnp.arange(LANES) < QK_ROPE)[:, None]
    wk_bot = jnp.broadcast_to(eye[:, None, :], (LANES, MLA_HEADS, HEAD_LANES))
    wkp = jnp.concatenate([wk_top, wk_bot], axis=0).reshape(KV_LORA + LANES, MLA_HEADS * HEAD_LANES).astype(BF16)
    wv = w_uv[e].reshape(KV_LORA, MLA_HEADS * V_HEAD).astype(BF16)
    wukp = jnp.concatenate([jnp.zeros((MLA_HEADS, HEAD_LANES - QK_NOPE, KV_LORA), F32),
                            jnp.transpose(w_uk[e], (1, 2, 0))], axis=1).astype(BF16)

    def block_diag(wb):
        n, k, _ = wb.shape
        eye_n = jnp.eye(n, dtype=F32)
        return (wb[:, :, None, :] * eye_n[:, None, :, None]).reshape(n * k, n * k).astype(BF16)

    row = lambda v: v.reshape(1, -1)
    return dict(
        win=win, gq=row(g_q[e]), gkv=row(g_kv[e]), wqa=wqa, wqb=wqb, wkp=wkp, wv=wv, wukp=wukp,
        conv_w=conv_w[e], conv_b=row(conv_b[e]), wa=block_diag(w_rg_a[e]), ba=row(b_rg_a[e]),
        wx=block_diag(w_rg_x[e]), bx=row(b_rg_x[e]), lam=row(lru_lambda[e]))


def _rope_tables(pos):
    half = QK_ROPE // 2
    inv = ROPE_THETA ** (-jnp.arange(half, dtype=F32) / half)
    ang = pos.astype(F32)[:, None] * inv
    cos, sin = jnp.cos(ang), jnp.sin(ang)
    n = pos.shape[0]
    z32 = jnp.zeros((n, HEAD_LANES - QK_NOPE - QK_ROPE), F32)
    ctab = jnp.concatenate([cos, cos, z32, jnp.ones((n, QK_NOPE), F32)], axis=1)
    stab = jnp.concatenate([-sin, sin, z32, jnp.zeros((n, QK_NOPE), F32)], axis=1)
    return ctab, stab


def _router_weights(w_rg, b_rg, w_re, b_re):
    wr = jnp.zeros((ROUTER_ROWS, D_MODEL), F32)
    wr = wr.at[:N_GROUPS].set(w_rg.T).at[EXPERT_ROW0:EXPERT_ROW0 + N_EXPERTS].set(w_re.T)
    br = jnp.zeros((ROUTER_ROWS, 1), F32)
    br = br.at[:N_GROUPS, 0].set(b_rg).at[EXPERT_ROW0:EXPERT_ROW0 + N_EXPERTS, 0].set(b_re)
    return wr.astype(BF16), br


def _run_group(x, c, pos, past, prep, tiles):
    depth = len(prep["layers"])
    alpha = (2.0 * depth) ** 0.25
    prompt = past is None
    g, r, d = x.shape
    mods = _adaln(c, prep["w_mod"], prep["b_mod"])
    ctab, stab = _rope_tables(pos)
    ckv_l, kpe_l, h_l, conv_l, k_l, v_l = [], [], [], [], [], []
    for l in range(depth):
        sh1, sc1, gt1, sh2, sc2, gt2 = mods[l]
        lw = prep["layers"][l]
        if l % 2 == 0:
            e = l // 2
            w0 = lw["mixer"]
            if prompt:
                q, kf, vv, ckv, kpe, u, gate = _ab_in(x, sc1, sh1, w0, ctab, stab, True, tiles["tok"])
                attn = _mla_prompt(q, kf, vv, tiles["attn"])
                h0 = jnp.zeros((g, 1, LRU_WIDTH), F32)
                conv_buf = jnp.zeros((g, CONV_W - 1, LRU_WIDTH), F32)
            else:
                cache_ckv, cache_kpe, st_h, st_conv, _, _, page_table = past
                q, ckv, kpe, u, gate = _ab_in(x, sc1, sh1, w0, ctab, stab, False, tiles["tok"])
                attn = _mla_sample(q, ckv, kpe, cache_ckv, cache_kpe, e, page_table, w0["wukp"], w0["wv"],
                                   tiles["mla_pages"])
                h0 = st_h[e][:, None, :]
                conv_buf = st_conv[e]
            lru_out, h_last, conv_new = _lru(u, gate, conv_buf, h0, w0, tiles["lru"])
            acts, ws = [attn, lru_out], [lw["w_out_a"], lw["w_out_b"]]
            ckv_l.append(ckv)
            kpe_l.append(kpe)
            h_l.append(h_last[:, 0, :])
            conv_l.append(conv_new)
        else:
            o = l // 2
            if prompt:
                k, v, qh, kh, vh = _sb_qkv(x, sc1, sh1, lw["w_qkv"], True, tiles["tok"])
                att = _sb_prompt(qh, kh, vh, tiles["attn"])
            else:
                k, v, q = _sb_qkv(x, sc1, sh1, lw["w_qkv"], False, tiles["tok"])
                att = _sb_sample(q, k, v, past[4], past[5], o, past[6], tiles["sb_pages"])
            acts, ws = [att], [lw["w_out"]]
            k_l.append(k.reshape(g, r, SB_HEADS, SB_HEAD_DIM))
            v_l.append(v.reshape(g, r, SB_HEADS, SB_HEAD_DIM))
        x1, h2, logits_t = _mix_out(acts, ws, x, gt1, lw["ln_g1"], lw["ln_b1"], sc2, sh2, lw["wr"], lw["br"],
                                    alpha, tiles["tok"])
        comb = _route(logits_t).T.reshape(g, r, N_EXPERTS)
        x = _moe(h2, comb, lw["w1"], lw["w3"], lw["w2"], x1, gt2, lw["ln_g2"], lw["ln_b2"], alpha, tiles["moe"])
    return x, (jnp.stack(ckv_l), jnp.stack(kpe_l), jnp.stack(h_l), jnp.stack(conv_l), jnp.stack(k_l),
               jnp.stack(v_l))


def kernel(x_prompt, x_sample, cache_mla_ckv, cache_mla_kpe, cache_sb_k, cache_sb_v, state_lru_h, state_conv,
           page_table, c_prompt, c_sample, w_mod, b_mod, ln_g, ln_b, w_in_ab, g_q, w_uq, g_kv, w_uk, w_uv,
           conv_w, conv_b, w_rg_a, b_rg_a, w_rg_x, b_rg_x, lru_lambda, w_out_ab, w_qkv_c, w_out_c,
           w_router_g, b_router_g, w_router_e, b_router_e, w_e1, w_e3, w_e2):
    depth = w_mod.shape[0]
    row = lambda v: v.reshape(1, -1)
    layers = []
    for l in range(depth):
        wr, br = _router_weights(w_router_g[l], b_router_g[l], w_router_e[l], b_router_e[l])
        lw = dict(ln_g1=row(ln_g[l, 0]), ln_b1=row(ln_b[l, 0]), ln_g2=row(ln_g[l, 1]), ln_b2=row(ln_b[l, 1]),
                  wr=wr, br=br, w1=w_e1[l].astype(BF16), w3=w_e3[l].astype(BF16), w2=w_e2[l].astype(BF16))
        if l % 2 == 0:
            e = l // 2
            lw["mixer"] = _prep_layer0(e, w_in_ab, g_q, w_uq, g_kv, w_uk, w_uv, conv_w, conv_b, w_rg_a, b_rg_a,
                                       w_rg_x, b_rg_x, lru_lambda)
            wo = w_out_ab[e].astype(BF16)
            lw["w_out_a"], lw["w_out_b"] = wo[:MLA_HEADS * V_HEAD], wo[MLA_HEADS * V_HEAD:]
        else:
            o = l // 2
            lw["w_qkv"] = w_qkv_c[o].astype(BF16)
            lw["w_out"] = w_out_c[o].astype(BF16)
        layers.append(lw)
    prep = dict(w_mod=w_mod, b_mod=b_mod, layers=layers)

    n_pool = cache_sb_k.shape[1]
    cache_k = cache_sb_k.reshape(cache_sb_k.shape[0], n_pool, PAGE_SIZE, D_MODEL)
    cache_v = cache_sb_v.reshape(cache_sb_v.shape[0], n_pool, PAGE_SIZE, D_MODEL)
    n_pages = page_table.shape[1]
    past = (cache_mla_ckv, cache_mla_kpe, state_lru_h, state_conv, cache_k, cache_v, page_table)

    seq = x_prompt.shape[1]
    tiles_p = dict(tok=min(256, seq), attn=min(512, seq), lru=min(512, seq), moe=min(512, seq))
    n_s = x_sample.shape[0] * x_sample.shape[1]
    tiles_s = dict(tok=min(256, n_s), lru=n_s, moe=min(512, n_s),
                   mla_pages=min(16, n_pages), sb_pages=min(8, n_pages))

    pos_p = jnp.arange(seq, dtype=jnp.int32)
    pos_s = n_pages * PAGE_SIZE + jnp.arange(x_sample.shape[1], dtype=jnp.int32)
    y_p, (ckv_p, kpe_p, h_p, conv_p, k_p, v_p) = _run_group(x_prompt, c_prompt, pos_p, None, prep, tiles_p)
    y_s, (ckv_s, kpe_s, h_s, conv_s, k_s, v_s) = _run_group(x_sample, c_sample, pos_s, past, prep, tiles_s)
    return (y_p, y_s, ckv_p, kpe_p, h_p, conv_p, k_p, v_p, ckv_s, kpe_s, h_s, conv_s, k_s, v_s)
```

```python
import functools

import numpy as np
import jax
import jax.numpy as jnp
from jax import lax
from jax.experimental import pallas as pl
from jax.experimental.pallas import tpu as pltpu

F32, BF16 = jnp.float32, jnp.bfloat16

D_MODEL = 1024
MLA_HEADS, QK_NOPE, QK_ROPE, V_HEAD = 8, 64, 32, 64
Q_LORA, KV_LORA = 384, 256
ROPE_THETA = 10000.0
MLA_SCALE = (QK_NOPE + QK_ROPE) ** -0.5
LRU_WIDTH, LRU_BLOCKS, CONV_W, LRU_C = 512, 8, 4, 8.0
SB_HEADS, SB_HEAD_DIM = 16, 64
SB_SCALE = SB_HEAD_DIM ** -0.5
N_GROUPS, EXP_PER_GROUP, N_EXPERTS, D_EXPERT = 4, 4, 16, 512
LN_EPS, RMS_EPS = 1e-5, 1e-6
NEG_INF = -1e30
PAGE_SIZE = 128

LANES = 128
SUBLANES = 8
VMEM_LIMIT_BYTES = 56 * 1024 * 1024

MXU_DIM = 256
SB_CUMSUM_KEYS = MXU_DIM
MOE_ROW_CHUNK = MXU_DIM

HEAD_LANES = LANES
ROUTER_ROWS = 128
EXPERT_ROW0 = 8


def _dot(a, b):
    return jnp.dot(a, b, preferred_element_type=F32)


def _dot_nt(a, b):
    return lax.dot_general(a, b, (((1,), (1,)), ((), ())), preferred_element_type=F32)


def _sigmoid(x):
    return 1.0 / (1.0 + jnp.exp(-x))


def _layer_norm(y, g, b):
    mu = jnp.mean(y, axis=-1, keepdims=True)
    d = y - mu
    var = jnp.mean(d * d, axis=-1, keepdims=True)
    return d * lax.rsqrt(var + LN_EPS) * g + b


def _rms_norm(x, g):
    return x * lax.rsqrt(jnp.mean(x * x, axis=-1, keepdims=True) + RMS_EPS) * g


def _shift_of(n):
    s = int(n).bit_length() - 1
    assert 1 << s == n
    return s


def _idiv(x, n):
    return lax.shift_right_logical(x, _shift_of(n))


def _imod(x, n):
    assert 1 << _shift_of(n) == n
    return x & (n - 1)


def _act_dtype(gb):
    return BF16 if gb == 1 else F32


def _lanes(x, width):
    n = width // LANES
    return x if n == 1 else jnp.concatenate([x] * n, axis=1)


def _params(*sem):
    return pltpu.CompilerParams(dimension_semantics=sem, vmem_limit_bytes=VMEM_LIMIT_BYTES)


def _token_tiles(groups, rows, target):
    if rows >= target:
        assert rows % target == 0
        return 1, target
    gb = min(groups, max(1, target // rows))
    assert groups % gb == 0
    return gb, rows


def _adaln_kernel(c_ref, w_ref, b_ref, o_ref):
    c = c_ref[...]
    s = (c * _sigmoid(c)).astype(BF16)
    o_ref[0] = _dot(s, w_ref[0].astype(BF16)) + b_ref[0]


def _adaln(c, w_mod, b_mod):
    depth, d, d6 = w_mod.shape
    b = c.shape[0]
    bp = -(-b // SUBLANES) * SUBLANES
    cp = jnp.pad(c, ((0, bp - b), (0, 0)))
    tn = 1536
    out = pl.pallas_call(
        _adaln_kernel,
        out_shape=jax.ShapeDtypeStruct((depth, bp, d6), F32),
        grid=(depth, d6 // tn),
        in_specs=[
            pl.BlockSpec((bp, d), lambda l, n: (0, 0)),
            pl.BlockSpec((1, d, tn), lambda l, n: (l, 0, n)),
            pl.BlockSpec((1, 1, tn), lambda l, n: (l, 0, n)),
        ],
        out_specs=pl.BlockSpec((1, bp, tn), lambda l, n: (l, 0, n)),
        compiler_params=_params("arbitrary", "arbitrary"),
        name="adaln",
    )(cp, w_mod, b_mod.reshape(depth, 1, d6))
    m = out[:, :b].reshape(depth, b, 6, 1, d)
    return [[m[l, :, i] for i in range(6)] for l in range(depth)]


def _ab_in_kernel(x_ref, sc_ref, sh_ref, win_ref, gq_ref, gkv_ref, wqa_ref, wqb_ref, ct_ref, st_ref,
                  *rest, prompt, gb, rb):
    if prompt:
        wkp_ref, wv_ref, q_ref, kf_ref, v_ref, ckv_ref, kpe_ref, u_ref, gate_ref = rest
    else:
        q_ref, ckv_ref, kpe_ref, u_ref, gate_ref = rest
    tm = gb * rb
    h = (x_ref[...] * (1.0 + sc_ref[...]) + sh_ref[...]).reshape(tm, D_MODEL).astype(BF16)
    p = _dot(h, win_ref[...])
    o_kv, o_u, o_g, o_ka, o_kb = Q_LORA, Q_LORA + KV_LORA, Q_LORA + KV_LORA + LRU_WIDTH, \
        Q_LORA + KV_LORA + 2 * LRU_WIDTH, Q_LORA + KV_LORA + 2 * LRU_WIDTH + LANES
    q_lat, kv_lat = p[:, :o_kv], p[:, o_kv:o_u]
    u_ref[...] = p[:, o_u:o_g].reshape(gb, rb, LRU_WIDTH)
    gate_ref[...] = p[:, o_g:o_ka].reshape(gb, rb, LRU_WIDTH)
    kpe_a, kpe_b = p[:, o_ka:o_kb], p[:, o_kb:o_kb + LANES]

    c = jnp.broadcast_to(ct_ref[...][None], (gb, rb, LANES)).reshape(tm, LANES)
    s = jnp.broadcast_to(st_ref[...][None], (gb, rb, LANES)).reshape(tm, LANES)

    qn = _rms_norm(q_lat, gq_ref[...]).astype(BF16)
    qa = _dot(qn, wqa_ref[...])
    qb = _dot(qn, wqb_ref[...])
    ckv = _rms_norm(kv_lat, gkv_ref[...])
    ckv_ref[...] = ckv.reshape(gb, rb, KV_LORA)
    kpe = kpe_a * c + kpe_b * s
    kpe_ref[...] = kpe[:, :QK_ROPE].reshape(gb, rb, QK_ROPE)

    for hd in range(MLA_HEADS):
        sl = slice(hd * HEAD_LANES, (hd + 1) * HEAD_LANES)
        qh = qa[:, sl] * c + qb[:, sl] * s
        if prompt:
            q_ref[0, hd] = qh.astype(BF16)
        else:
            q_ref[:, :, sl] = qh.reshape(gb, rb, HEAD_LANES)
    if prompt:
        ckv_bf = ckv.astype(BF16)
        kin = jnp.concatenate([ckv_bf, kpe.astype(BF16)], axis=-1)
        kf = _dot(kin, wkp_ref[...])
        vv = _dot(ckv_bf, wv_ref[...])
        for hd in range(MLA_HEADS):
            kf_ref[0, hd] = kf[:, hd * HEAD_LANES:(hd + 1) * HEAD_LANES].astype(BF16)
        for hp in range(MLA_HEADS // 2):
            v_ref[0, hp] = vv[:, hp * LANES:(hp + 1) * LANES].astype(BF16)


def _ab_in(x, sc, sh, wts, ctab, stab, prompt, tm_target):
    g, r, d = x.shape
    gb, rb = _token_tiles(g, r, tm_target)
    grid = (g // gb, r // rb)
    full = lambda a: pl.BlockSpec(a.shape, lambda i, j, _n=a.ndim: (0,) * _n)
    tok = lambda c: pl.BlockSpec((gb, rb, c), lambda i, j: (i, j, 0))
    mod = pl.BlockSpec((gb, 1, d), lambda i, j: (i, 0, 0))
    tab = pl.BlockSpec((rb, LANES), lambda i, j: (j, 0))
    ins = [x, sc, sh, wts["win"], wts["gq"], wts["gkv"], wts["wqa"], wts["wqb"], ctab, stab]
    in_specs = [tok(d), mod, mod, full(wts["win"]), full(wts["gq"]), full(wts["gkv"]),
                full(wts["wqa"]), full(wts["wqb"]), tab, tab]
    tail_shapes = [jax.ShapeDtypeStruct((g, r, KV_LORA), F32), jax.ShapeDtypeStruct((g, r, QK_ROPE), F32),
                   jax.ShapeDtypeStruct((g, r, LRU_WIDTH), F32), jax.ShapeDtypeStruct((g, r, LRU_WIDTH), F32)]
    tail_specs = [tok(KV_LORA), tok(QK_ROPE), tok(LRU_WIDTH), tok(LRU_WIDTH)]
    if prompt:
        assert gb == 1
        ins += [wts["wkp"], wts["wv"]]
        in_specs += [full(wts["wkp"]), full(wts["wv"])]
        hm = lambda nh: pl.BlockSpec((1, nh, rb, LANES), lambda i, j: (i, 0, j, 0))
        out_shape = [jax.ShapeDtypeStruct((g, MLA_HEADS, r, LANES), BF16),
                     jax.ShapeDtypeStruct((g, MLA_HEADS, r, LANES), BF16),
                     jax.ShapeDtypeStruct((g, MLA_HEADS // 2, r, LANES), BF16)] + tail_shapes
        out_specs = [hm(MLA_HEADS), hm(MLA_HEADS), hm(MLA_HEADS // 2)] + tail_specs
    else:
        out_shape = [jax.ShapeDtypeStruct((g, r, MLA_HEADS * HEAD_LANES), F32)] + tail_shapes
        out_specs = [tok(MLA_HEADS * HEAD_LANES)] + tail_specs
    return pl.pallas_call(
        functools.partial(_ab_in_kernel, prompt=prompt, gb=gb, rb=rb),
        out_shape=out_shape, grid=grid, in_specs=in_specs, out_specs=out_specs,
        compiler_params=_params("arbitrary", "arbitrary"),
        name="ab_in_prompt" if prompt else "ab_in_sample",
    )(*ins)


def _triangle(nq, descending):
    qi, kj = [], []
    for q in range(nq):
        ks = range(q, -1, -1) if descending else range(q + 1)
        for k in ks:
            qi.append(q)
            kj.append(k)
    return jnp.asarray(np.array(qi, np.int32)), jnp.asarray(np.array(kj, np.int32))


def _mla_prompt_kernel(qi_ref, kj_ref, q_ref, k_ref, v_ref, o_ref, m_scr, l_scr, acc_scr, *, tq):
    t = pl.program_id(1)
    qi, kj = qi_ref[t], kj_ref[t]

    @pl.when(kj == 0)
    def _init():
        m_scr[...] = jnp.full(m_scr.shape, NEG_INF, F32)
        l_scr[...] = jnp.zeros(l_scr.shape, F32)
        acc_scr[...] = jnp.zeros(acc_scr.shape, F32)

    def run(diagonal):
        if diagonal:
            row = lax.broadcasted_iota(jnp.int32, (tq, tq), 0)
            col = lax.broadcasted_iota(jnp.int32, (tq, tq), 1)
            valid = col <= row

        def body(hp, carry):
            vv = v_ref[0, hp]
            hds = [2 * hp, 2 * hp + 1]
            ss = [_dot_nt(q_ref[0, hd], k_ref[0, hd]) * MLA_SCALE for hd in hds]
            if diagonal:
                ss = [jnp.where(valid, s, NEG_INF) for s in ss]
            ps, alphas = [], []
            for hd, s in zip(hds, ss):
                m_prev = m_scr[hd]
                m_new = jnp.maximum(m_prev, jnp.max(s, axis=-1, keepdims=True))
                alpha = jnp.exp(m_prev - m_new)
                p = jnp.exp(s - _lanes(m_new, tq))
                l_scr[hd] = alpha * l_scr[hd] + jnp.sum(p, axis=-1, keepdims=True)
                m_scr[hd] = m_new
                ps.append(p.astype(BF16))
                alphas.append(alpha)
            for hd, p, alpha in zip(hds, ps, alphas):
                acc_scr[hd] = alpha * acc_scr[hd] + _dot(p, vv)
            return carry

        lax.fori_loop(0, MLA_HEADS // 2, body, 0)

    @pl.when(kj < qi)
    def _off():
        run(False)

    @pl.when(kj == qi)
    def _diag():
        run(True)
        lane = lax.broadcasted_iota(jnp.int32, (tq, LANES), 1)
        for hp in range(MLA_HEADS // 2):
            o0 = acc_scr[2 * hp] / l_scr[2 * hp]
            o1 = acc_scr[2 * hp + 1] / l_scr[2 * hp + 1]
            o_ref[0, :, hp * LANES:(hp + 1) * LANES] = jnp.where(lane < V_HEAD, o0, o1).astype(BF16)


def _mla_prompt(q, k, v, tq):
    b, nh, s, _ = q.shape
    nq = s // tq
    qi, kj = _triangle(nq, descending=False)
    grid_spec = pltpu.PrefetchScalarGridSpec(
        num_scalar_prefetch=2, grid=(b, qi.shape[0]),
        in_specs=[
            pl.BlockSpec((1, nh, tq, LANES), lambda bb, t, qi_r, kj_r: (bb, 0, qi_r[t], 0)),
            pl.BlockSpec((1, nh, tq, LANES), lambda bb, t, qi_r, kj_r: (bb, 0, kj_r[t], 0)),
            pl.BlockSpec((1, nh // 2, tq, LANES), lambda bb, t, qi_r, kj_r: (bb, 0, kj_r[t], 0)),
        ],
        out_specs=pl.BlockSpec((1, tq, nh * V_HEAD), lambda bb, t, qi_r, kj_r: (bb, qi_r[t], 0)),
        scratch_shapes=[pltpu.VMEM((nh, tq, LANES), F32), pltpu.VMEM((nh, tq, LANES), F32),
                        pltpu.VMEM((nh, tq, LANES), F32)],
    )
    return pl.pallas_call(
        functools.partial(_mla_prompt_kernel, tq=tq),
        out_shape=jax.ShapeDtypeStruct((b, s, nh * V_HEAD), BF16),
        grid_spec=grid_spec,
        compiler_params=_params("arbitrary", "arbitrary"),
        name="mla_prompt",
    )(qi, kj, q, k, v)


def _mla_sample_kernel(pt_ref, q_ref, cn_ref, kn_ref, wuk_ref, wuv_ref, *rest, pages, rows):
    c_refs, k_refs = rest[:pages], rest[pages:2 * pages]
    o_ref, qabs_scr, qpe_scr, m_scr, l_scr, acc_scr = rest[2 * pages:]
    j = pl.program_id(1)
    nrow = MLA_HEADS * rows

    def update(scores, values):
        smax = scores[0]
        for s in scores[1:]:
            smax = jnp.maximum(smax, s)
        m_prev = m_scr[...]
        m_new = jnp.maximum(m_prev, jnp.max(smax, axis=-1, keepdims=True))
        alpha = jnp.exp(m_prev - m_new)
        ps = [jnp.exp(s - _lanes(m_new, s.shape[1])) for s in scores]
        psum = ps[0]
        for p in ps[1:]:
            psum = psum + p
        l_scr[...] = alpha * l_scr[...] + jnp.sum(psum, axis=-1, keepdims=True)
        pv = None
        for p, c in zip(ps, values):
            d = _dot(p.astype(BF16), c)
            pv = d if pv is None else pv + d
        acc_scr[...] = _lanes(alpha, KV_LORA) * acc_scr[...] + pv
        m_scr[...] = m_new

    @pl.when(j == 0)
    def _first():
        q = q_ref[0]
        q64 = jnp.concatenate([q[:, hd * HEAD_LANES:(hd + 1) * HEAD_LANES] for hd in range(MLA_HEADS)],
                              axis=0)
        q64_bf = q64.astype(BF16)
        rgrp = _idiv(lax.broadcasted_iota(jnp.int32, (nrow, KV_LORA), 0), rows)
        qabs = jnp.zeros((nrow, KV_LORA), F32)
        for hd in range(MLA_HEADS):
            qabs = qabs + jnp.where(rgrp == hd, _dot(q64_bf, wuk_ref[hd]), 0.0)
        qabs_scr[...] = qabs.astype(BF16)
        qpe_scr[...] = q64_bf[:, :QK_ROPE]
        m_scr[...] = jnp.full(m_scr.shape, NEG_INF, F32)
        l_scr[...] = jnp.zeros(l_scr.shape, F32)
        acc_scr[...] = jnp.zeros(acc_scr.shape, F32)
        pad = PAGE_SIZE - rows
        cn = jnp.concatenate([cn_ref[0], jnp.zeros((pad, KV_LORA), F32)], axis=0).astype(BF16)
        kn = jnp.concatenate([kn_ref[0], jnp.zeros((pad, QK_ROPE), F32)], axis=0).astype(BF16)
        row = lax.broadcasted_iota(jnp.int32, (nrow, PAGE_SIZE), 0)
        col = lax.broadcasted_iota(jnp.int32, (nrow, PAGE_SIZE), 1)
        s = (_dot_nt(qabs_scr[...], cn) + _dot_nt(qpe_scr[...], kn)) * MLA_SCALE
        update([jnp.where(col <= _imod(row, rows), s, NEG_INF)], [cn])

    qabs, qpe = qabs_scr[...], qpe_scr[...]
    cs = [jnp.concatenate([c_refs[i][0, 0], c_refs[i + 1][0, 0]], axis=0).astype(BF16)
          for i in range(0, pages, 2)]
    ks = [jnp.concatenate([k_refs[i][0, 0], k_refs[i + 1][0, 0]], axis=1).astype(BF16)
          for i in range(0, pages, 2)]
    update([(_dot_nt(qabs, c) + _dot(qpe, k)) * MLA_SCALE for c, k in zip(cs, ks)], cs)

    @pl.when(j == pl.num_programs(1) - 1)
    def _last():
        o_lat = (acc_scr[...] / _lanes(l_scr[...], KV_LORA)).astype(BF16)
        o_full = _dot(o_lat, wuv_ref[...])
        lane_h = _idiv(lax.broadcasted_iota(jnp.int32, (rows, MLA_HEADS * V_HEAD), 1), V_HEAD)
        out = jnp.zeros((rows, MLA_HEADS * V_HEAD), F32)
        for hd in range(MLA_HEADS):
            out = out + jnp.where(lane_h == hd, o_full[hd * rows:(hd + 1) * rows], 0.0)
        o_ref[0] = out


def _mla_sample(q, ckv_new, kpe_new, cache_ckv, cache_kpe, e, page_table, wukp, wuv, pages):
    b, rows, _ = q.shape
    n_pages = page_table.shape[1]
    assert n_pages % pages == 0 and pages % 2 == 0 and rows == SUBLANES
    steps = n_pages // pages
    pt = page_table.reshape(-1)
    nrow = MLA_HEADS * rows

    def page_spec(shape, i):
        return pl.BlockSpec((1, 1) + shape,
                            lambda bb, j, pt_r, _i=i: (e, pt_r[bb * n_pages + j * pages + _i], 0, 0))

    cache_kpe_t = jnp.transpose(cache_kpe, (0, 1, 3, 2))
    full = lambda a: pl.BlockSpec(a.shape, lambda bb, j, pt_r, _n=a.ndim: (0,) * _n)
    seq = lambda c: pl.BlockSpec((1, rows, c), lambda bb, j, pt_r: (bb, 0, 0))
    grid_spec = pltpu.PrefetchScalarGridSpec(
        num_scalar_prefetch=1, grid=(b, steps),
        in_specs=[seq(q.shape[-1]), seq(KV_LORA), seq(QK_ROPE), full(wukp), full(wuv)]
        + [page_spec((PAGE_SIZE, KV_LORA), i) for i in range(pages)]
        + [page_spec((QK_ROPE, PAGE_SIZE), i) for i in range(pages)],
        out_specs=seq(MLA_HEADS * V_HEAD),
        scratch_shapes=[pltpu.VMEM((nrow, KV_LORA), BF16), pltpu.VMEM((nrow, QK_ROPE), BF16),
                        pltpu.VMEM((nrow, LANES), F32), pltpu.VMEM((nrow, LANES), F32),
                        pltpu.VMEM((nrow, KV_LORA), F32)],
    )
    return pl.pallas_call(
        functools.partial(_mla_sample_kernel, pages=pages, rows=rows),
        out_shape=jax.ShapeDtypeStruct((b, rows, MLA_HEADS * V_HEAD), F32),
        grid_spec=grid_spec,
        compiler_params=_params("arbitrary", "arbitrary"),
        name="mla_sample",
    )(pt, q, ckv_new, kpe_new, wukp, wuv, *([cache_ckv] * pages), *([cache_kpe_t] * pages))


def _gelu_tanh(x):
    return x * (0.5 * (1.0 + jnp.tanh(0.7978845608028654 * (x + 0.044715 * (x * x * x)))))


def _lru_kernel(u_ref, gate_ref, cbuf_ref, h0_ref, cw_ref, cb_ref, wa_ref, ba_ref, wx_ref, bx_ref, lam_ref,
                out_ref, hlast_ref, cnew_ref, ext_scr, a_scr, b_scr, h_scr, hc_scr, *, gb, rb):
    j = pl.program_id(1)
    tm, w = gb * rb, LRU_WIDTH
    pre = SUBLANES
    tail = CONV_W - 1

    @pl.when(j == 0)
    def _first():
        ext_scr[:, 0:pre, :] = jnp.zeros((gb, pre, w), F32)
        ext_scr[:, pre - tail:pre, :] = cbuf_ref[...]
        hc_scr[...] = h0_ref[...]

    ext_scr[:, pre:pre + rb, :] = u_ref[...]
    uc = cb_ref[...][None]
    for t in range(CONV_W):
        uc = uc + ext_scr[:, pre - tail + t:pre - tail + t + rb, :] * cw_ref[t:t + 1, :][None]
    new_tail = ext_scr[:, pre + rb - tail:pre + rb, :]
    ext_scr[:, pre - tail:pre, :] = new_tail
    cnew_ref[...] = new_tail

    uc = uc.reshape(tm, w)
    ub = uc.astype(BF16)
    r = _sigmoid(_dot(ub, wa_ref[...]) + ba_ref[...])
    i = _sigmoid(_dot(ub, wx_ref[...]) + bx_ref[...])
    nl = -lam_ref[...]
    softplus = jnp.maximum(nl, 0.0) + jnp.log(1.0 + jnp.exp(-jnp.abs(nl)))
    log_a = (-LRU_C) * r * softplus
    a = jnp.exp(log_a)
    bt = jnp.sqrt(1.0 - jnp.exp(2.0 * log_a)) * (i * uc)

    rin = _imod(lax.broadcasted_iota(jnp.int32, (tm, w), 0), SUBLANES)
    for sft in (1, 2, 4):
        a_sh = pltpu.roll(a, sft, 0)
        b_sh = pltpu.roll(bt, sft, 0)
        m = rin >= sft
        bt = jnp.where(m, a * b_sh + bt, bt)
        a = jnp.where(m, a * a_sh, a)

    if rb == SUBLANES:
        hseq = a.reshape(gb, rb, w) * hc_scr[...] + bt.reshape(gb, rb, w)
        hc_scr[...] = hseq[:, rb - 1:rb, :]
    else:
        a_scr[...] = a
        b_scr[...] = bt

        def body(g, hprev):
            r0 = pl.multiple_of(g * SUBLANES, SUBLANES)
            hh = a_scr[pl.ds(r0, SUBLANES), :] * hprev + b_scr[pl.ds(r0, SUBLANES), :]
            h_scr[pl.ds(r0, SUBLANES), :] = hh
            return hh[SUBLANES - 1:SUBLANES, :]

        hc_scr[0] = lax.fori_loop(0, rb // SUBLANES, body, hc_scr[0])
        hseq = h_scr[...].reshape(gb, rb, w)
    hlast_ref[...] = hc_scr[...]
    out_ref[...] = (hseq * _gelu_tanh(gate_ref[...])).astype(out_ref.dtype)


def _lru(u, gate, conv_buf, h0, wts, tm_target):
    g, r, w = u.shape
    gb, rb = _token_tiles(g, r, tm_target)
    assert gb == 1 or rb == SUBLANES
    tm = gb * rb
    full = lambda a: pl.BlockSpec(a.shape, lambda i, j, _n=a.ndim: (0,) * _n)
    tok = pl.BlockSpec((gb, rb, w), lambda i, j: (i, j, 0))
    per_g = lambda rows: pl.BlockSpec((gb, rows, w), lambda i, j: (i, 0, 0))
    names = ["conv_w", "conv_b", "wa", "ba", "wx", "bx", "lam"]
    return pl.pallas_call(
        functools.partial(_lru_kernel, gb=gb, rb=rb),
        out_shape=[jax.ShapeDtypeStruct((g, r, w), _act_dtype(gb)), jax.ShapeDtypeStruct((g, 1, w), F32),
                   jax.ShapeDtypeStruct((g, CONV_W - 1, w), F32)],
        grid=(g // gb, r // rb),
        in_specs=[tok, tok, per_g(CONV_W - 1), per_g(1)] + [full(wts[n]) for n in names],
        out_specs=[tok, per_g(1), per_g(CONV_W - 1)],
        scratch_shapes=[pltpu.VMEM((gb, SUBLANES + rb, w), F32), pltpu.VMEM((tm, w), F32),
                        pltpu.VMEM((tm, w), F32), pltpu.VMEM((tm, w), F32), pltpu.VMEM((gb, 1, w), F32)],
        compiler_params=_params("arbitrary", "arbitrary"),
        name="lru",
    )(u, gate, conv_buf, h0, *[wts[n] for n in names])


def _mix_out_kernel(*refs, n_in, gb, rb, alpha):
    a_refs, w_refs = refs[:n_in], refs[n_in:2 * n_in]
    x_ref, gt_ref, lng_ref, lnb_ref, sc2_ref, sh2_ref, wr_ref, br_ref, x1_ref, h2_ref, lg_ref = refs[2 * n_in:]
    tm = gb * rb
    mix = None
    for a_ref, w_ref in zip(a_refs, w_refs):
        part = _dot(a_ref[...].reshape(tm, a_ref.shape[-1]).astype(BF16), w_ref[...])
        mix = part if mix is None else mix + part
    y = alpha * x_ref[...] + (1.0 + gt_ref[...]) * mix.reshape(gb, rb, D_MODEL)
    x1 = _layer_norm(y, lng_ref[...], lnb_ref[...])
    x1_ref[...] = x1
    h2 = x1 * (1.0 + sc2_ref[...]) + sh2_ref[...]
    h2_ref[...] = h2.astype(h2_ref.dtype)
    lg_ref[...] = _dot_nt(wr_ref[...], h2.reshape(tm, D_MODEL).astype(BF16)) + br_ref[...]


def _mix_out(acts, ws, x, gt, lng, lnb, sc2, sh2, wr, br, alpha, tm_target):
    g, r, d = x.shape
    gb, rb = _token_tiles(g, r, tm_target)
    tm = gb * rb
    nj = r // rb
    full = lambda a: pl.BlockSpec(a.shape, lambda i, j, _n=a.ndim: (0,) * _n)
    tok = lambda c: pl.BlockSpec((gb, rb, c), lambda i, j: (i, j, 0))
    mod = pl.BlockSpec((gb, 1, d), lambda i, j: (i, 0, 0))
    return pl.pallas_call(
        functools.partial(_mix_out_kernel, n_in=len(acts), gb=gb, rb=rb, alpha=alpha),
        out_shape=[jax.ShapeDtypeStruct((g, r, d), F32), jax.ShapeDtypeStruct((g, r, d), _act_dtype(gb)),
                   jax.ShapeDtypeStruct((ROUTER_ROWS, g * r), F32)],
        grid=(g // gb, nj),
        in_specs=[tok(a.shape[-1]) for a in acts] + [full(w) for w in ws]
        + [tok(d), mod, full(lng), full(lnb), mod, mod, full(wr), full(br)],
        out_specs=[tok(d), tok(d), pl.BlockSpec((ROUTER_ROWS, tm), lambda i, j: (0, i * nj + j))],
        compiler_params=_params("arbitrary", "arbitrary"),
        name="mix_out",
    )(*acts, *ws, x, gt, lng, lnb, sc2, sh2, wr, br)


def _route_kernel(lg_ref, comb_ref):
    tn = lg_ref.shape[1]
    big = 3.0e38
    grow = lax.broadcasted_iota(jnp.int32, (SUBLANES, tn), 0)
    gl = jnp.where(grow < N_GROUPS, lg_ref[0:SUBLANES, :], -big)
    gmax = jnp.max(gl, axis=0, keepdims=True)
    g_idx = jnp.min(jnp.where(gl == gmax, grow, N_GROUPS), axis=0, keepdims=True)
    g_w = 1.0 / jnp.sum(jnp.where(grow < N_GROUPS, jnp.exp(gl - gmax), 0.0), axis=0, keepdims=True)

    el = lg_ref[EXPERT_ROW0:EXPERT_ROW0 + N_EXPERTS, :]
    erow = lax.broadcasted_iota(jnp.int32, (N_EXPERTS, tn), 0)
    ingrp = _idiv(erow, EXP_PER_GROUP) == g_idx
    emax = jnp.max(jnp.where(ingrp, el, -big), axis=0, keepdims=True)
    ex = jnp.where(ingrp, jnp.exp(el - emax), 0.0)
    p = ex / jnp.sum(ex, axis=0, keepdims=True)
    ps = jnp.where(ingrp, p, -1.0)
    p1 = jnp.max(ps, axis=0, keepdims=True)
    i1 = jnp.min(jnp.where(ps == p1, erow, N_EXPERTS), axis=0, keepdims=True)
    ps2 = jnp.where(erow == i1, -1.0, ps)
    p2 = jnp.max(ps2, axis=0, keepdims=True)
    i2 = jnp.min(jnp.where(ps2 == p2, erow, N_EXPERTS), axis=0, keepdims=True)
    tot = p1 + p2
    comb_ref[...] = (jnp.where(erow == i1, g_w * p1 / tot, 0.0)
                     + jnp.where(erow == i2, g_w * p2 / tot, 0.0))


def _route(logits_t):
    n = logits_t.shape[1]
    tn = min(n, 2048)
    return pl.pallas_call(
        _route_kernel,
        out_shape=jax.ShapeDtypeStruct((N_EXPERTS, n), F32),
        grid=(n // tn,),
        in_specs=[pl.BlockSpec((ROUTER_ROWS, tn), lambda i: (0, i))],
        out_specs=pl.BlockSpec((N_EXPERTS, tn), lambda i: (0, i)),
        compiler_params=_params("arbitrary"),
        name="route",
    )(logits_t)


def _moe_kernel(h_ref, comb_ref, w1_ref, w3_ref, w2_ref, x1_ref, gt_ref, lng_ref, lnb_ref, out_ref, acc_scr,
                *, gb, rb, alpha):
    e = pl.program_id(2)
    tm = gb * rb

    @pl.when(e == 0)
    def _init():
        acc_scr[...] = jnp.zeros(acc_scr.shape, F32)

    h = h_ref[...].reshape(tm, D_MODEL).astype(BF16)
    comb = comb_ref[...].reshape(tm, N_EXPERTS)
    lane = lax.broadcasted_iota(jnp.int32, (tm, N_EXPERTS), 1)
    ce = jnp.sum(jnp.where(lane == e, comb, 0.0), axis=1, keepdims=True)
    rc = min(tm, MOE_ROW_CHUNK)
    chunks = [slice(r0, r0 + rc) for r0 in range(0, tm, rc)]
    w1, w3, w2 = w1_ref[0], w3_ref[0], w2_ref[0]
    ab = [(_dot(h[c], w1), _dot(h[c], w3)) for c in chunks]
    hid = [((a * _sigmoid(a)) * b * ce[c]).astype(BF16) for (a, b), c in zip(ab, chunks)]
    for hd, c in zip(hid, chunks):
        acc_scr[c, :] += _dot(hd, w2)

    @pl.when(e == pl.num_programs(2) - 1)
    def _fin():
        y = alpha * x1_ref[...] + (1.0 + gt_ref[...]) * acc_scr[...].reshape(gb, rb, D_MODEL)
        out_ref[...] = _layer_norm(y, lng_ref[...], lnb_ref[...])


def _moe(h2, comb, w1, w3, w2, x1, gt, lng, lnb, alpha, tm_target):
    g, r, d = x1.shape
    gb, rb = _token_tiles(g, r, tm_target)
    ne = w1.shape[0]
    tok = lambda c: pl.BlockSpec((gb, rb, c), lambda i, j, e: (i, j, 0))
    full = lambda a: pl.BlockSpec(a.shape, lambda i, j, e, _n=a.ndim: (0,) * _n)
    wspec = lambda a: pl.BlockSpec((1,) + a.shape[1:], lambda i, j, e: (e, 0, 0))
    return pl.pallas_call(
        functools.partial(_moe_kernel, gb=gb, rb=rb, alpha=alpha),
        out_shape=jax.ShapeDtypeStruct((g, r, d), F32),
        grid=(g // gb, r // rb, ne),
        in_specs=[tok(d), tok(N_EXPERTS), wspec(w1), wspec(w3), wspec(w2), tok(d),
                  pl.BlockSpec((gb, 1, d), lambda i, j, e: (i, 0, 0)), full(lng), full(lnb)],
        out_specs=tok(d),
        scratch_shapes=[pltpu.VMEM((gb * rb, d), F32)],
        compiler_params=_params("arbitrary", "arbitrary", "arbitrary"),
        name="moe",
    )(h2, comb, w1, w3, w2, x1, gt, lng, lnb)


def _sb_qkv_kernel(x_ref, sc_ref, sh_ref, w_ref, *outs, prompt, gb, rb):
    tm = gb * rb
    h = (x_ref[...] * (1.0 + sc_ref[...]) + sh_ref[...]).reshape(tm, D_MODEL).astype(BF16)
    qkv = _dot(h, w_ref[...])
    q = qkv[:, :D_MODEL] * SB_SCALE
    k = qkv[:, D_MODEL:2 * D_MODEL]
    v = qkv[:, 2 * D_MODEL:]
    if prompt:
        k_ref, v_ref, qh_ref, kh_ref, vh_ref = outs
        for hp in range(SB_HEADS // 2):
            sl = slice(hp * LANES, (hp + 1) * LANES)
            qh_ref[0, hp] = q[:, sl].astype(BF16)
            kh_ref[0, hp] = k[:, sl].astype(BF16)
            vh_ref[0, hp] = v[:, sl].astype(BF16)
    else:
        k_ref, v_ref, q_ref = outs
        q_ref[...] = q.reshape(gb, rb, D_MODEL)
    k_ref[...] = k.reshape(gb, rb, D_MODEL)
    v_ref[...] = v.reshape(gb, rb, D_MODEL)


def _sb_qkv(x, sc, sh, w, prompt, tm_target):
    g, r, d = x.shape
    gb, rb = _token_tiles(g, r, tm_target)
    tok = pl.BlockSpec((gb, rb, d), lambda i, j: (i, j, 0))
    mod = pl.BlockSpec((gb, 1, d), lambda i, j: (i, 0, 0))
    out_shape = [jax.ShapeDtypeStruct((g, r, d), F32)] * 2
    out_specs = [tok, tok]
    if prompt:
        assert gb == 1
        npair = SB_HEADS // 2
        out_shape += [jax.ShapeDtypeStruct((g, npair, r, LANES), BF16)] * 3
        out_specs += [pl.BlockSpec((1, npair, rb, LANES), lambda i, j: (i, 0, j, 0))] * 3
    else:
        out_shape += [jax.ShapeDtypeStruct((g, r, d), F32)]
        out_specs += [tok]
    return pl.pallas_call(
        functools.partial(_sb_qkv_kernel, prompt=prompt, gb=gb, rb=rb),
        out_shape=out_shape, grid=(g // gb, r // rb),
        in_specs=[tok, mod, mod, pl.BlockSpec(w.shape, lambda i, j: (0, 0))],
        out_specs=out_specs,
        compiler_params=_params("arbitrary", "arbitrary"),
        name="sb_qkv_prompt" if prompt else "sb_qkv_sample",
    )(x, sc, sh, w)


def _sb_terms(z, upper, valid):
    lq = jnp.minimum(z, 0.0) - jnp.log(1.0 + jnp.exp(-jnp.abs(z)))
    lk = lq - z
    if valid is not None:
        lk = jnp.where(valid, lk, 0.0)
    hi = lk.astype(BF16)
    lo = (lk - hi.astype(F32)).astype(BF16)
    return lq + (_dot(hi, upper) + _dot(lo, upper)), jnp.sum(lk, axis=-1, keepdims=True)


def _sb_weights(arg, carry, valid):
    w = jnp.exp(arg + _lanes(carry, arg.shape[1]))
    if valid is not None:
        w = jnp.where(valid, w, 0.0)
    return w.astype(BF16)


def _sb_prompt_kernel(qi_ref, kj_ref, q_ref, k_ref, v_ref, up_ref, o_ref, carry_scr, acc_scr, *, tq):
    t = pl.program_id(1)
    qi, kj = qi_ref[t], kj_ref[t]
    npair = SB_HEADS // 2

    @pl.when(kj == qi)
    def _init():
        carry_scr[...] = jnp.zeros(carry_scr.shape, F32)
        acc_scr[...] = jnp.zeros(acc_scr.shape, F32)

    kb = up_ref.shape[0]

    def run(diagonal):
        valid = None
        if diagonal:
            row = lax.broadcasted_iota(jnp.int32, (tq, tq), 0)
            col = lax.broadcasted_iota(jnp.int32, (tq, tq), 1)
            valid = col < row
        lane = lax.broadcasted_iota(jnp.int32, (tq, LANES), 1)
        upper = up_ref[...]

        def body(hp, c):
            qq, kk, vv = q_ref[0, hp].astype(F32), k_ref[0, hp], v_ref[0, hp]
            hsels = [lane < SB_HEAD_DIM, lane >= SB_HEAD_DIM]
            zs = [_dot_nt(jnp.where(hsel, qq, 0.0).astype(BF16), kk) for hsel in hsels]
            k0s = list(range(tq - kb, -1, -kb))
            vlds = [None if valid is None else valid[:, k0:k0 + kb] for k0 in k0s]
            terms = [[_sb_terms(z[:, k0:k0 + kb], upper, vld) for k0, vld in zip(k0s, vlds)] for z in zs]
            out = None
            for sub in range(2):
                carry = carry_scr[hp, sub]
                pv = None
                for (arg, total), k0, vld in zip(terms[sub], k0s, vlds):
                    d = _dot(_sb_weights(arg, carry, vld), vv[k0:k0 + kb])
                    pv = d if pv is None else pv + d
                    carry = carry + total
                carry_scr[hp, sub] = carry
                out = jnp.where(hsels[sub], pv, 0.0 if out is None else out)
            acc_scr[hp] += out
            return c

        lax.fori_loop(0, npair, body, 0)

    @pl.when(kj == qi)
    def _diag():
        run(True)

    @pl.when(kj < qi)
    def _off():
        run(False)

    @pl.when(kj == 0)
    def _fin():
        for hp in range(npair):
            o_ref[0, :, hp * LANES:(hp + 1) * LANES] = acc_scr[hp].astype(BF16)


def _upper(n):
    j = np.arange(n)[:, None]
    s = np.arange(n)[None, :]
    return jnp.asarray((j > s).astype(np.float32), dtype=BF16)


def _sb_prompt(q, k, v, tq):
    b, npair, s, _ = q.shape
    nq = s // tq
    qi, kj = _triangle(nq, descending=True)
    kb = SB_CUMSUM_KEYS if tq % SB_CUMSUM_KEYS == 0 else tq
    up = _upper(kb)
    blk = lambda sel: pl.BlockSpec((1, npair, tq, LANES), sel)
    grid_spec = pltpu.PrefetchScalarGridSpec(
        num_scalar_prefetch=2, grid=(b, qi.shape[0]),
        in_specs=[blk(lambda bb, t, qi_r, kj_r: (bb, 0, qi_r[t], 0)),
                  blk(lambda bb, t, qi_r, kj_r: (bb, 0, kj_r[t], 0)),
                  blk(lambda bb, t, qi_r, kj_r: (bb, 0, kj_r[t], 0)),
                  pl.BlockSpec((kb, kb), lambda bb, t, qi_r, kj_r: (0, 0))],
        out_specs=pl.BlockSpec((1, tq, npair * LANES), lambda bb, t, qi_r, kj_r: (bb, qi_r[t], 0)),
        scratch_shapes=[pltpu.VMEM((npair, 2, tq, LANES), F32), pltpu.VMEM((npair, tq, LANES), F32)],
    )
    return pl.pallas_call(
        functools.partial(_sb_prompt_kernel, tq=tq),
        out_shape=jax.ShapeDtypeStruct((b, s, npair * LANES), BF16),
        grid_spec=grid_spec,
        compiler_params=_params("arbitrary", "arbitrary"),
        name="sb_prompt",
    )(qi, kj, q, k, v, up)


def _sb_sample_kernel(pt_ref, q_ref, kn_ref, vn_ref, up_ref, *rest, pages, rows):
    k_refs, v_refs = rest[:pages], rest[pages:2 * pages]
    o_ref, qbd_scr, carry_scr, acc_scr = rest[2 * pages:]
    j = pl.program_id(1)
    nrow = SB_HEADS * rows
    upper = up_ref[...]

    @pl.when(j == 0)
    def _first():
        qt = jnp.concatenate([q_ref[0]] * SB_HEADS, axis=0)
        rgrp = _idiv(lax.broadcasted_iota(jnp.int32, (nrow, D_MODEL), 0), rows)
        lgrp = _idiv(lax.broadcasted_iota(jnp.int32, (nrow, D_MODEL), 1), SB_HEAD_DIM)
        qbd = jnp.where(rgrp == lgrp, qt, 0.0).astype(BF16)
        qbd_scr[...] = qbd
        pad = jnp.zeros((PAGE_SIZE - rows, D_MODEL), F32)
        kn = jnp.concatenate([kn_ref[0], pad], axis=0).astype(BF16)
        vn = jnp.concatenate([vn_ref[0], pad], axis=0).astype(BF16)
        row = lax.broadcasted_iota(jnp.int32, (nrow, PAGE_SIZE), 0)
        col = lax.broadcasted_iota(jnp.int32, (nrow, PAGE_SIZE), 1)
        valid = col < _imod(row, rows)
        arg, total = _sb_terms(_dot_nt(qbd, kn), upper[:PAGE_SIZE, :PAGE_SIZE], valid)
        w = _sb_weights(arg, jnp.zeros((nrow, LANES), F32), valid)
        carry_scr[...] = jnp.broadcast_to(total, (nrow, LANES))
        acc_scr[...] = _dot(w, vn)

    qbd = qbd_scr[...]
    pairs = range(0, pages, 2)
    zs = [_dot(qbd, jnp.concatenate([k_refs[i + 1][0, 0], k_refs[i][0, 0]], axis=1).astype(BF16))
          for i in pairs]
    terms = [_sb_terms(z, upper, None) for z in zs]
    carry = carry_scr[...]
    pv = None
    for i, (arg, total) in zip(pairs, terms):
        vt = jnp.concatenate([v_refs[i + 1][0, 0], v_refs[i][0, 0]], axis=1).astype(BF16)
        d = _dot_nt(_sb_weights(arg, carry, None), vt)
        pv = d if pv is None else pv + d
        carry = carry + total
    carry_scr[...] = carry
    acc_scr[...] += pv

    @pl.when(j == pl.num_programs(1) - 1)
    def _last():
        lgrp = _idiv(lax.broadcasted_iota(jnp.int32, (rows, D_MODEL), 1), SB_HEAD_DIM)
        out = jnp.zeros((rows, D_MODEL), F32)
        for hd in range(SB_HEADS):
            out = out + jnp.where(lgrp == hd, acc_scr[hd * rows:(hd + 1) * rows, :], 0.0)
        o_ref[0] = out


def _sb_sample(q, k_new, v_new, cache_k, cache_v, o_idx, page_table, pages):
    b, rows, d = q.shape
    n_pages = page_table.shape[1]
    assert n_pages % pages == 0 and pages % 2 == 0 and rows == SUBLANES
    steps = n_pages // pages
    pt = page_table.reshape(-1)
    nrow = SB_HEADS * rows
    up = _upper(2 * PAGE_SIZE)

    def page_spec(i):
        return pl.BlockSpec(
            (1, 1, d, PAGE_SIZE),
            lambda bb, j, pt_r, _i=i: (o_idx, pt_r[bb * n_pages + n_pages - 1 - (j * pages + _i)], 0, 0))

    seq = pl.BlockSpec((1, rows, d), lambda bb, j, pt_r: (bb, 0, 0))
    grid_spec = pltpu.PrefetchScalarGridSpec(
        num_scalar_prefetch=1, grid=(b, steps),
        in_specs=[seq, seq, seq, pl.BlockSpec(up.shape, lambda bb, j, pt_r: (0, 0))]
        + [page_spec(i) for i in range(pages)] * 2,
        out_specs=seq,
        scratch_shapes=[pltpu.VMEM((nrow, d), BF16), pltpu.VMEM((nrow, LANES), F32), pltpu.VMEM((nrow, d), F32)],
    )
    return pl.pallas_call(
        functools.partial(_sb_sample_kernel, pages=pages, rows=rows),
        out_shape=jax.ShapeDtypeStruct((b, rows, d), F32),
        grid_spec=grid_spec,
        compiler_params=_params("arbitrary", "arbitrary"),
        name="sb_sample",
    )(pt, q, k_new, v_new, up, *([cache_k] * pages), *([cache_v] * pages))


def _prep_layer0(e, w_in_ab, g_q, w_uq, g_kv, w_uk, w_uv, conv_w, conv_b, w_rg_a, b_rg_a, w_rg_x, b_rg_x,
                 lru_lambda):
    half = QK_ROPE // 2
    w_in = w_in_ab[e]
    o1, o2, o3, o4 = Q_LORA, Q_LORA + KV_LORA, Q_LORA + KV_LORA + QK_ROPE, Q_LORA + KV_LORA + QK_ROPE + LRU_WIDTH
    kpe_w = w_in[:, o2:o3]
    zpad = jnp.zeros((D_MODEL, LANES - QK_ROPE), F32)
    win = jnp.concatenate([
        w_in[:, :o2], w_in[:, o3:o4], w_in[:, o4:],
        kpe_w, zpad,
        kpe_w[:, half:], kpe_w[:, :half], zpad], axis=1).astype(BF16)

    wq = w_uq[e].reshape(Q_LORA, MLA_HEADS, QK_NOPE + QK_ROPE)
    nope, x1, x2 = wq[..., :QK_NOPE], wq[..., QK_NOPE:QK_NOPE + half], wq[..., QK_NOPE + half:]
    z32 = jnp.zeros((Q_LORA, MLA_HEADS, HEAD_LANES - QK_NOPE - QK_ROPE), F32)
    wqa = jnp.concatenate([x1, x2, z32, nope], axis=-1).reshape(Q_LORA, MLA_HEADS * HEAD_LANES).astype(BF16)
    wqb = jnp.concatenate([x2, x1, z32, jnp.zeros_like(nope)], axis=-1)
    wqb = wqb.reshape(Q_LORA, MLA_HEADS * HEAD_LANES).astype(BF16)

    wk_top = jnp.concatenate([jnp.zeros((KV_LORA, MLA_HEADS, HEAD_LANES - QK_NOPE), F32), w_uk[e]], axis=-1)
    eye = jnp.eye(LANES, HEAD_LANES, dtype=F32) * (jnp.arange(LANES) < QK_ROPE)[:, None]
    wk_bot = jnp.broadcast_to(eye[:, None, :], (LANES, MLA_HEADS, HEAD_LANES))
    wkp = jnp.concatenate([wk_top, wk_bot], axis=0).reshape(KV_LORA + LANES, MLA_HEADS * HEAD_LANES).astype(BF16)
    wv = w_uv[e].reshape(KV_LORA, MLA_HEADS * V_HEAD).astype(BF16)
    wukp = jnp.concatenate([jnp.zeros((MLA_HEADS, HEAD_LANES - QK_NOPE, KV_LORA), F32),
                            jnp.transpose(w_uk[e], (1, 2, 0))], axis=1).astype(BF16)

    def block_diag(wb):
        n, k, _ = wb.shape
        eye_n = jnp.eye(n, dtype=F32)
        return (wb[:, :, None, :] * eye_n[:, None, :, None]).reshape(n * k, n * k).astype(BF16)

    row = lambda v: v.reshape(1, -1)
    return dict(
        win=win, gq=row(g_q[e]), gkv=row(g_kv[e]), wqa=wqa, wqb=wqb, wkp=wkp, wv=wv, wukp=wukp,
        conv_w=conv_w[e], conv_b=row(conv_b[e]), wa=block_diag(w_rg_a[e]), ba=row(b_rg_a[e]),
        wx=block_diag(w_rg_x[e]), bx=row(b_rg_x[e]), lam=row(lru_lambda[e]))


def _rope_tables(pos):
    half = QK_ROPE // 2
    inv = ROPE_THETA ** (-jnp.arange(half, dtype=F32) / half)
    ang = pos.astype(F32)[:, None] * inv
    cos, sin = jnp.cos(ang), jnp.sin(ang)
    n = pos.shape[0]
    z32 = jnp.zeros((n, HEAD_LANES - QK_NOPE - QK_ROPE), F32)
    ctab = jnp.concatenate([cos, cos, z32, jnp.ones((n, QK_NOPE), F32)], axis=1)
    stab = jnp.concatenate([-sin, sin, z32, jnp.zeros((n, QK_NOPE), F32)], axis=1)
    return ctab, stab


def _router_weights(w_rg, b_rg, w_re, b_re):
    wr = jnp.zeros((ROUTER_ROWS, D_MODEL), F32)
    wr = wr.at[:N_GROUPS].set(w_rg.T).at[EXPERT_ROW0:EXPERT_ROW0 + N_EXPERTS].set(w_re.T)
    br = jnp.zeros((ROUTER_ROWS, 1), F32)
    br = br.at[:N_GROUPS, 0].set(b_rg).at[EXPERT_ROW0:EXPERT_ROW0 + N_EXPERTS, 0].set(b_re)
    return wr.astype(BF16), br


def _run_group(x, c, pos, past, prep, tiles):
    depth = len(prep["layers"])
    alpha = (2.0 * depth) ** 0.25
    prompt = past is None
    g, r, d = x.shape
    mods = _adaln(c, prep["w_mod"], prep["b_mod"])
    ctab, stab = _rope_tables(pos)
    ckv_l, kpe_l, h_l, conv_l, k_l, v_l = [], [], [], [], [], []
    for l in range(depth):
        sh1, sc1, gt1, sh2, sc2, gt2 = mods[l]
        lw = prep["layers"][l]
        if l % 2 == 0:
            e = l // 2
            w0 = lw["mixer"]
            if prompt:
                q, kf, vv, ckv, kpe, u, gate = _ab_in(x, sc1, sh1, w0, ctab, stab, True, tiles["tok"])
                attn = _mla_prompt(q, kf, vv, tiles["attn"])
                h0 = jnp.zeros((g, 1, LRU_WIDTH), F32)
                conv_buf = jnp.zeros((g, CONV_W - 1, LRU_WIDTH), F32)
            else:
                cache_ckv, cache_kpe, st_h, st_conv, _, _, page_table = past
                q, ckv, kpe, u, gate = _ab_in(x, sc1, sh1, w0, ctab, stab, False, tiles["tok"])
                attn = _mla_sample(q, ckv, kpe, cache_ckv, cache_kpe, e, page_table, w0["wukp"], w0["wv"],
                                   tiles["mla_pages"])
                h0 = st_h[e][:, None, :]
                conv_buf = st_conv[e]
            lru_out, h_last, conv_new = _lru(u, gate, conv_buf, h0, w0, tiles["lru"])
            acts, ws = [attn, lru_out], [lw["w_out_a"], lw["w_out_b"]]
            ckv_l.append(ckv)
            kpe_l.append(kpe)
            h_l.append(h_last[:, 0, :])
            conv_l.append(conv_new)
        else:
            o = l // 2
            if prompt:
                k, v, qh, kh, vh = _sb_qkv(x, sc1, sh1, lw["w_qkv"], True, tiles["tok"])
                att = _sb_prompt(qh, kh, vh, tiles["attn"])
            else:
                k, v, q = _sb_qkv(x, sc1, sh1, lw["w_qkv"], False, tiles["tok"])
                att = _sb_sample(q, k, v, past[4], past[5], o, past[6], tiles["sb_pages"])
            acts, ws = [att], [lw["w_out"]]
            k_l.append(k.reshape(g, r, SB_HEADS, SB_HEAD_DIM))
            v_l.append(v.reshape(g, r, SB_HEADS, SB_HEAD_DIM))
        x1, h2, logits_t = _mix_out(acts, ws, x, gt1, lw["ln_g1"], lw["ln_b1"], sc2, sh2, lw["wr"], lw["br"],
                                    alpha, tiles["tok"])
        comb = _route(logits_t).T.reshape(g, r, N_EXPERTS)
        x = _moe(h2, comb, lw["w1"], lw["w3"], lw["w2"], x1, gt2, lw["ln_g2"], lw["ln_b2"], alpha, tiles["moe"])
    return x, (jnp.stack(ckv_l), jnp.stack(kpe_l), jnp.stack(h_l), jnp.stack(conv_l), jnp.stack(k_l),
               jnp.stack(v_l))


def kernel(x_prompt, x_sample, cache_mla_ckv, cache_mla_kpe, cache_sb_k, cache_sb_v, state_lru_h, state_conv,
           page_table, c_prompt, c_sample, w_mod, b_mod, ln_g, ln_b, w_in_ab, g_q, w_uq, g_kv, w_uk, w_uv,
           conv_w, conv_b, w_rg_a, b_rg_a, w_rg_x, b_rg_x, lru_lambda, w_out_ab, w_qkv_c, w_out_c,
           w_router_g, b_router_g, w_router_e, b_router_e, w_e1, w_e3, w_e2):
    depth = w_mod.shape[0]
    row = lambda v: v.reshape(1, -1)
    layers = []
    for l in range(depth):
        wr, br = _router_weights(w_router_g[l], b_router_g[l], w_router_e[l], b_router_e[l])
        lw = dict(ln_g1=row(ln_g[l, 0]), ln_b1=row(ln_b[l, 0]), ln_g2=row(ln_g[l, 1]), ln_b2=row(ln_b[l, 1]),
                  wr=wr, br=br, w1=w_e1[l].astype(BF16), w3=w_e3[l].astype(BF16), w2=w_e2[l].astype(BF16))
        if l % 2 == 0:
            e = l // 2
            lw["mixer"] = _prep_layer0(e, w_in_ab, g_q, w_uq, g_kv, w_uk, w_uv, conv_w, conv_b, w_rg_a, b_rg_a,
                                       w_rg_x, b_rg_x, lru_lambda)
            wo = w_out_ab[e].astype(BF16)
            lw["w_out_a"], lw["w_out_b"] = wo[:MLA_HEADS * V_HEAD], wo[MLA_HEADS * V_HEAD:]
        else:
            o = l // 2
            lw["w_qkv"] = w_qkv_c[o].astype(BF16)
            lw["w_out"] = w_out_c[o].astype(BF16)
        layers.append(lw)
    prep = dict(w_mod=w_mod, b_mod=b_mod, layers=layers)

    n_pool = cache_sb_k.shape[1]
    cache_k = jnp.transpose(cache_sb_k, (0, 1, 3, 4, 2)).reshape(cache_sb_k.shape[0], n_pool, D_MODEL, PAGE_SIZE)
    cache_v = jnp.transpose(cache_sb_v, (0, 1, 3, 4, 2)).reshape(cache_sb_v.shape[0], n_pool, D_MODEL, PAGE_SIZE)
    n_pages = page_table.shape[1]
    past = (cache_mla_ckv, cache_mla_kpe, state_lru_h, state_conv, cache_k, cache_v, page_table)

    seq = x_prompt.shape[1]
    tiles_p = dict(tok=min(256, seq), attn=min(512, seq), lru=min(512, seq), moe=min(1024, seq))
    n_s = x_sample.shape[0] * x_sample.shape[1]
    tiles_s = dict(tok=min(256, n_s), lru=n_s, moe=min(1024, n_s),
                   mla_pages=min(32, n_pages), sb_pages=min(8, n_pages))

    pos_p = jnp.arange(seq, dtype=jnp.int32)
    pos_s = n_pages * PAGE_SIZE + jnp.arange(x_sample.shape[1], dtype=jnp.int32)
    y_p, (ckv_p, kpe_p, h_p, conv_p, k_p, v_p) = _run_group(x_prompt, c_prompt, pos_p, None, prep, tiles_p)
    y_s, (ckv_s, kpe_s, h_s, conv_s, k_s, v_s) = _run_group(x_sample, c_sample, pos_s, past, prep, tiles_s)
    return (y_p, y_s, ckv_p, kpe_p, h_p, conv_p, k_p, v_p, ckv_s, kpe_s, h_s, conv_s, k_s, v_s)
```

```python
import functools

import numpy as np
import jax
import jax.numpy as jnp
from jax import lax
from jax.experimental import pallas as pl
from jax.experimental.pallas import tpu as pltpu

F32, BF16 = jnp.float32, jnp.bfloat16

D_MODEL = 1024
MLA_HEADS, QK_NOPE, QK_ROPE, V_HEAD = 8, 64, 32, 64
Q_LORA, KV_LORA = 384, 256
ROPE_THETA = 10000.0
MLA_SCALE = (QK_NOPE + QK_ROPE) ** -0.5
LRU_WIDTH, LRU_BLOCKS, CONV_W, LRU_C = 512, 8, 4, 8.0
SB_HEADS, SB_HEAD_DIM = 16, 64
SB_SCALE = SB_HEAD_DIM ** -0.5
N_GROUPS, EXP_PER_GROUP, N_EXPERTS, D_EXPERT = 4, 4, 16, 512
LN_EPS, RMS_EPS = 1e-5, 1e-6
NEG_INF = -1e30
PAGE_SIZE = 128

LANES = 128
SUBLANES = 8
VMEM_LIMIT_BYTES = 56 * 1024 * 1024

MXU_DIM = 256
SB_CUMSUM_KEYS = MXU_DIM
MOE_ROW_CHUNK = MXU_DIM
MOE_EXPERTS_PER_STEP = 4
SB_PAIRS_PER_TRIP = 2

HEAD_LANES = LANES
ROUTER_ROWS = 128
EXPERT_ROW0 = 8


def _dot(a, b):
    return jnp.dot(a, b, preferred_element_type=F32)


def _dot_nt(a, b):
    return lax.dot_general(a, b, (((1,), (1,)), ((), ())), preferred_element_type=F32)


def _sigmoid(x):
    return 1.0 / (1.0 + jnp.exp(-x))


def _layer_norm(y, g, b):
    mu = jnp.mean(y, axis=-1, keepdims=True)
    d = y - mu
    var = jnp.mean(d * d, axis=-1, keepdims=True)
    return d * lax.rsqrt(var + LN_EPS) * g + b


def _rms_norm(x, g):
    return x * lax.rsqrt(jnp.mean(x * x, axis=-1, keepdims=True) + RMS_EPS) * g


def _shift_of(n):
    s = int(n).bit_length() - 1
    assert 1 << s == n
    return s


def _idiv(x, n):
    return lax.shift_right_logical(x, _shift_of(n))


def _imod(x, n):
    assert 1 << _shift_of(n) == n
    return x & (n - 1)


def _act_dtype(gb):
    return BF16 if gb == 1 else F32


def _lanes(x, width):
    n = width // LANES
    return x if n == 1 else jnp.concatenate([x] * n, axis=1)


def _params(*sem):
    return pltpu.CompilerParams(dimension_semantics=sem, vmem_limit_bytes=VMEM_LIMIT_BYTES)


def _token_tiles(groups, rows, target):
    if rows >= target:
        assert rows % target == 0
        return 1, target
    gb = min(groups, max(1, target // rows))
    assert groups % gb == 0
    return gb, rows


def _adaln_kernel(c_ref, w_ref, b_ref, o_ref):
    c = c_ref[...]
    s = (c * _sigmoid(c)).astype(BF16)
    o_ref[0] = _dot(s, w_ref[0].astype(BF16)) + b_ref[0]


def _adaln(c, w_mod, b_mod):
    depth, d, d6 = w_mod.shape
    b = c.shape[0]
    bp = -(-b // SUBLANES) * SUBLANES
    cp = jnp.pad(c, ((0, bp - b), (0, 0)))
    tn = 1536
    out = pl.pallas_call(
        _adaln_kernel,
        out_shape=jax.ShapeDtypeStruct((depth, bp, d6), F32),
        grid=(depth, d6 // tn),
        in_specs=[
            pl.BlockSpec((bp, d), lambda l, n: (0, 0)),
            pl.BlockSpec((1, d, tn), lambda l, n: (l, 0, n)),
            pl.BlockSpec((1, 1, tn), lambda l, n: (l, 0, n)),
        ],
        out_specs=pl.BlockSpec((1, bp, tn), lambda l, n: (l, 0, n)),
        compiler_params=_params("arbitrary", "arbitrary"),
        name="adaln",
    )(cp, w_mod, b_mod.reshape(depth, 1, d6))
    m = out[:, :b].reshape(depth, b, 6, 1, d)
    return [[m[l, :, i] for i in range(6)] for l in range(depth)]


def _ab_in_kernel(x_ref, sc_ref, sh_ref, win_ref, gq_ref, gkv_ref, wqa_ref, wqb_ref, ct_ref, st_ref,
                  *rest, prompt, gb, rb):
    if prompt:
        wkp_ref, wv_ref, q_ref, kf_ref, v_ref, ckv_ref, kpe_ref, u_ref, gate_ref = rest
    else:
        q_ref, ckv_ref, kpe_ref, u_ref, gate_ref = rest
    tm = gb * rb
    h = (x_ref[...] * (1.0 + sc_ref[...]) + sh_ref[...]).reshape(tm, D_MODEL).astype(BF16)
    p = _dot(h, win_ref[...])
    o_kv, o_u, o_g, o_ka, o_kb = Q_LORA, Q_LORA + KV_LORA, Q_LORA + KV_LORA + LRU_WIDTH, \
        Q_LORA + KV_LORA + 2 * LRU_WIDTH, Q_LORA + KV_LORA + 2 * LRU_WIDTH + LANES
    q_lat, kv_lat = p[:, :o_kv], p[:, o_kv:o_u]
    u_ref[...] = p[:, o_u:o_g].reshape(gb, rb, LRU_WIDTH)
    gate_ref[...] = p[:, o_g:o_ka].reshape(gb, rb, LRU_WIDTH)
    kpe_a, kpe_b = p[:, o_ka:o_kb], p[:, o_kb:o_kb + LANES]

    c = jnp.broadcast_to(ct_ref[...][None], (gb, rb, LANES)).reshape(tm, LANES)
    s = jnp.broadcast_to(st_ref[...][None], (gb, rb, LANES)).reshape(tm, LANES)

    qn = _rms_norm(q_lat, gq_ref[...]).astype(BF16)
    qa = _dot(qn, wqa_ref[...])
    qb = _dot(qn, wqb_ref[...])
    ckv = _rms_norm(kv_lat, gkv_ref[...])
    ckv_ref[...] = ckv.reshape(gb, rb, KV_LORA)
    kpe = kpe_a * c + kpe_b * s
    kpe_ref[...] = kpe[:, :QK_ROPE].reshape(gb, rb, QK_ROPE)

    for hd in range(MLA_HEADS):
        sl = slice(hd * HEAD_LANES, (hd + 1) * HEAD_LANES)
        qh = qa[:, sl] * c + qb[:, sl] * s
        if prompt:
            q_ref[0, hd] = qh.astype(BF16)
        else:
            q_ref[:, :, sl] = qh.reshape(gb, rb, HEAD_LANES)
    if prompt:
        ckv_bf = ckv.astype(BF16)
        kin = jnp.concatenate([ckv_bf, kpe.astype(BF16)], axis=-1)
        kf = _dot(kin, wkp_ref[...])
        vv = _dot(ckv_bf, wv_ref[...])
        for hd in range(MLA_HEADS):
            kf_ref[0, hd] = kf[:, hd * HEAD_LANES:(hd + 1) * HEAD_LANES].astype(BF16)
        for hp in range(MLA_HEADS // 2):
            v_ref[0, hp] = vv[:, hp * LANES:(hp + 1) * LANES].astype(BF16)


def _ab_in(x, sc, sh, wts, ctab, stab, prompt, tm_target):
    g, r, d = x.shape
    gb, rb = _token_tiles(g, r, tm_target)
    grid = (g // gb, r // rb)
    full = lambda a: pl.BlockSpec(a.shape, lambda i, j, _n=a.ndim: (0,) * _n)
    tok = lambda c: pl.BlockSpec((gb, rb, c), lambda i, j: (i, j, 0))
    mod = pl.BlockSpec((gb, 1, d), lambda i, j: (i, 0, 0))
    tab = pl.BlockSpec((rb, LANES), lambda i, j: (j, 0))
    ins = [x, sc, sh, wts["win"], wts["gq"], wts["gkv"], wts["wqa"], wts["wqb"], ctab, stab]
    in_specs = [tok(d), mod, mod, full(wts["win"]), full(wts["gq"]), full(wts["gkv"]),
                full(wts["wqa"]), full(wts["wqb"]), tab, tab]
    tail_shapes = [jax.ShapeDtypeStruct((g, r, KV_LORA), F32), jax.ShapeDtypeStruct((g, r, QK_ROPE), F32),
                   jax.ShapeDtypeStruct((g, r, LRU_WIDTH), F32), jax.ShapeDtypeStruct((g, r, LRU_WIDTH), F32)]
    tail_specs = [tok(KV_LORA), tok(QK_ROPE), tok(LRU_WIDTH), tok(LRU_WIDTH)]
    if prompt:
        assert gb == 1
        ins += [wts["wkp"], wts["wv"]]
        in_specs += [full(wts["wkp"]), full(wts["wv"])]
        hm = lambda nh: pl.BlockSpec((1, nh, rb, LANES), lambda i, j: (i, 0, j, 0))
        out_shape = [jax.ShapeDtypeStruct((g, MLA_HEADS, r, LANES), BF16),
                     jax.ShapeDtypeStruct((g, MLA_HEADS, r, LANES), BF16),
                     jax.ShapeDtypeStruct((g, MLA_HEADS // 2, r, LANES), BF16)] + tail_shapes
        out_specs = [hm(MLA_HEADS), hm(MLA_HEADS), hm(MLA_HEADS // 2)] + tail_specs
    else:
        out_shape = [jax.ShapeDtypeStruct((g, r, MLA_HEADS * HEAD_LANES), F32)] + tail_shapes
        out_specs = [tok(MLA_HEADS * HEAD_LANES)] + tail_specs
    return pl.pallas_call(
        functools.partial(_ab_in_kernel, prompt=prompt, gb=gb, rb=rb),
        out_shape=out_shape, grid=grid, in_specs=in_specs, out_specs=out_specs,
        compiler_params=_params("arbitrary", "arbitrary"),
        name="ab_in_prompt" if prompt else "ab_in_sample",
    )(*ins)


def _triangle(nq, descending):
    qi, kj = [], []
    for q in range(nq):
        ks = range(q, -1, -1) if descending else range(q + 1)
        for k in ks:
            qi.append(q)
            kj.append(k)
    return jnp.asarray(np.array(qi, np.int32)), jnp.asarray(np.array(kj, np.int32))


def _mla_prompt_kernel(qi_ref, kj_ref, q_ref, k_ref, v_ref, o_ref, m_scr, l_scr, acc_scr, *, tq):
    t = pl.program_id(1)
    qi, kj = qi_ref[t], kj_ref[t]

    @pl.when(kj == 0)
    def _init():
        m_scr[...] = jnp.full(m_scr.shape, NEG_INF, F32)
        l_scr[...] = jnp.zeros(l_scr.shape, F32)
        acc_scr[...] = jnp.zeros(acc_scr.shape, F32)

    def run(diagonal):
        if diagonal:
            row = lax.broadcasted_iota(jnp.int32, (tq, tq), 0)
            col = lax.broadcasted_iota(jnp.int32, (tq, tq), 1)
            valid = col <= row

        def body(i, carry):
            hds = [4 * i + n for n in range(4)]
            ss = [_dot_nt(q_ref[0, hd], k_ref[0, hd]) * MLA_SCALE for hd in hds]
            if diagonal:
                ss = [jnp.where(valid, s, NEG_INF) for s in ss]
            ps, alphas = [], []
            for hd, s in zip(hds, ss):
                m_prev = m_scr[hd]
                m_new = jnp.maximum(m_prev, jnp.max(s, axis=-1, keepdims=True))
                alpha = jnp.exp(m_prev - m_new)
                p = jnp.exp(s - _lanes(m_new, tq))
                l_scr[hd] = alpha * l_scr[hd] + jnp.sum(p, axis=-1, keepdims=True)
                m_scr[hd] = m_new
                ps.append(p.astype(BF16))
                alphas.append(alpha)
            for n, (hd, p, alpha) in enumerate(zip(hds, ps, alphas)):
                acc_scr[hd] = alpha * acc_scr[hd] + _dot(p, v_ref[0, 2 * i + n // 2])
            return carry

        lax.fori_loop(0, MLA_HEADS // 4, body, 0)

    @pl.when(kj < qi)
    def _off():
        run(False)

    @pl.when(kj == qi)
    def _diag():
        run(True)
        lane = lax.broadcasted_iota(jnp.int32, (tq, LANES), 1)
        for hp in range(MLA_HEADS // 2):
            o0 = acc_scr[2 * hp] / l_scr[2 * hp]
            o1 = acc_scr[2 * hp + 1] / l_scr[2 * hp + 1]
            o_ref[0, :, hp * LANES:(hp + 1) * LANES] = jnp.where(lane < V_HEAD, o0, o1).astype(BF16)


def _mla_prompt(q, k, v, tq):
    b, nh, s, _ = q.shape
    nq = s // tq
    qi, kj = _triangle(nq, descending=False)
    grid_spec = pltpu.PrefetchScalarGridSpec(
        num_scalar_prefetch=2, grid=(b, qi.shape[0]),
        in_specs=[
            pl.BlockSpec((1, nh, tq, LANES), lambda bb, t, qi_r, kj_r: (bb, 0, qi_r[t], 0)),
            pl.BlockSpec((1, nh, tq, LANES), lambda bb, t, qi_r, kj_r: (bb, 0, kj_r[t], 0)),
            pl.BlockSpec((1, nh // 2, tq, LANES), lambda bb, t, qi_r, kj_r: (bb, 0, kj_r[t], 0)),
        ],
        out_specs=pl.BlockSpec((1, tq, nh * V_HEAD), lambda bb, t, qi_r, kj_r: (bb, qi_r[t], 0)),
        scratch_shapes=[pltpu.VMEM((nh, tq, LANES), F32), pltpu.VMEM((nh, tq, LANES), F32),
                        pltpu.VMEM((nh, tq, LANES), F32)],
    )
    return pl.pallas_call(
        functools.partial(_mla_prompt_kernel, tq=tq),
        out_shape=jax.ShapeDtypeStruct((b, s, nh * V_HEAD), BF16),
        grid_spec=grid_spec,
        compiler_params=_params("arbitrary", "arbitrary"),
        name="mla_prompt",
    )(qi, kj, q, k, v)


def _mla_sample_kernel(pt_ref, q_ref, cn_ref, kn_ref, wuk_ref, wuv_ref, *rest, pages, rows):
    c_refs, k_refs = rest[:pages], rest[pages:2 * pages]
    o_ref, qabs_scr, qpe_scr, m_scr, l_scr, acc_scr = rest[2 * pages:]
    j = pl.program_id(1)
    nrow = MLA_HEADS * rows

    def update(scores, values):
        smax = scores[0]
        for s in scores[1:]:
            smax = jnp.maximum(smax, s)
        m_prev = m_scr[...]
        m_new = jnp.maximum(m_prev, jnp.max(smax, axis=-1, keepdims=True))
        alpha = jnp.exp(m_prev - m_new)
        ps = [jnp.exp(s - _lanes(m_new, s.shape[1])) for s in scores]
        psum = ps[0]
        for p in ps[1:]:
            psum = psum + p
        l_scr[...] = alpha * l_scr[...] + jnp.sum(psum, axis=-1, keepdims=True)
        pv = None
        for p, c in zip(ps, values):
            d = _dot(p.astype(BF16), c)
            pv = d if pv is None else pv + d
        acc_scr[...] = _lanes(alpha, KV_LORA) * acc_scr[...] + pv
        m_scr[...] = m_new

    @pl.when(j == 0)
    def _first():
        q = q_ref[0]
        q64 = jnp.concatenate([q[:, hd * HEAD_LANES:(hd + 1) * HEAD_LANES] for hd in range(MLA_HEADS)],
                              axis=0)
        q64_bf = q64.astype(BF16)
        rgrp = _idiv(lax.broadcasted_iota(jnp.int32, (nrow, KV_LORA), 0), rows)
        qabs = jnp.zeros((nrow, KV_LORA), F32)
        for hd in range(MLA_HEADS):
            qabs = qabs + jnp.where(rgrp == hd, _dot(q64_bf, wuk_ref[hd]), 0.0)
        qabs_scr[...] = qabs.astype(BF16)
        qpe_scr[...] = q64_bf[:, :QK_ROPE]
        m_scr[...] = jnp.full(m_scr.shape, NEG_INF, F32)
        l_scr[...] = jnp.zeros(l_scr.shape, F32)
        acc_scr[...] = jnp.zeros(acc_scr.shape, F32)
        pad = PAGE_SIZE - rows
        cn = jnp.concatenate([cn_ref[0], jnp.zeros((pad, KV_LORA), F32)], axis=0).astype(BF16)
        kn = jnp.concatenate([kn_ref[0], jnp.zeros((pad, QK_ROPE), F32)], axis=0).astype(BF16)
        row = lax.broadcasted_iota(jnp.int32, (nrow, PAGE_SIZE), 0)
        col = lax.broadcasted_iota(jnp.int32, (nrow, PAGE_SIZE), 1)
        s = (_dot_nt(qabs_scr[...], cn) + _dot_nt(qpe_scr[...], kn)) * MLA_SCALE
        update([jnp.where(col <= _imod(row, rows), s, NEG_INF)], [cn])

    qabs, qpe = qabs_scr[...], qpe_scr[...]
    cs = [jnp.concatenate([c_refs[i][0, 0], c_refs[i + 1][0, 0]], axis=0).astype(BF16)
          for i in range(0, pages, 2)]
    ks = [jnp.concatenate([k_refs[i][0, 0], k_refs[i + 1][0, 0]], axis=1).astype(BF16)
          for i in range(0, pages, 2)]
    update([(_dot_nt(qabs, c) + _dot(qpe, k)) * MLA_SCALE for c, k in zip(cs, ks)], cs)

    @pl.when(j == pl.num_programs(1) - 1)
    def _last():
        o_lat = (acc_scr[...] / _lanes(l_scr[...], KV_LORA)).astype(BF16)
        o_full = _dot(o_lat, wuv_ref[...])
        lane_h = _idiv(lax.broadcasted_iota(jnp.int32, (rows, MLA_HEADS * V_HEAD), 1), V_HEAD)
        out = jnp.zeros((rows, MLA_HEADS * V_HEAD), F32)
        for hd in range(MLA_HEADS):
            out = out + jnp.where(lane_h == hd, o_full[hd * rows:(hd + 1) * rows], 0.0)
        o_ref[0] = out


def _mla_sample(q, ckv_new, kpe_new, cache_ckv, cache_kpe, e, page_table, wukp, wuv, pages):
    b, rows, _ = q.shape
    n_pages = page_table.shape[1]
    assert n_pages % pages == 0 and pages % 2 == 0 and rows == SUBLANES
    steps = n_pages // pages
    pt = page_table.reshape(-1)
    nrow = MLA_HEADS * rows

    def page_spec(shape, i):
        return pl.BlockSpec((1, 1) + shape,
                            lambda bb, j, pt_r, _i=i: (e, pt_r[bb * n_pages + j * pages + _i], 0, 0))

    cache_kpe_t = jnp.transpose(cache_kpe, (0, 1, 3, 2))
    full = lambda a: pl.BlockSpec(a.shape, lambda bb, j, pt_r, _n=a.ndim: (0,) * _n)
    seq = lambda c: pl.BlockSpec((1, rows, c), lambda bb, j, pt_r: (bb, 0, 0))
    grid_spec = pltpu.PrefetchScalarGridSpec(
        num_scalar_prefetch=1, grid=(b, steps),
        in_specs=[seq(q.shape[-1]), seq(KV_LORA), seq(QK_ROPE), full(wukp), full(wuv)]
        + [page_spec((PAGE_SIZE, KV_LORA), i) for i in range(pages)]
        + [page_spec((QK_ROPE, PAGE_SIZE), i) for i in range(pages)],
        out_specs=seq(MLA_HEADS * V_HEAD),
        scratch_shapes=[pltpu.VMEM((nrow, KV_LORA), BF16), pltpu.VMEM((nrow, QK_ROPE), BF16),
                        pltpu.VMEM((nrow, LANES), F32), pltpu.VMEM((nrow, LANES), F32),
                        pltpu.VMEM((nrow, KV_LORA), F32)],
    )
    return pl.pallas_call(
        functools.partial(_mla_sample_kernel, pages=pages, rows=rows),
        out_shape=jax.ShapeDtypeStruct((b, rows, MLA_HEADS * V_HEAD), F32),
        grid_spec=grid_spec,
        compiler_params=_params("arbitrary", "arbitrary"),
        name="mla_sample",
    )(pt, q, ckv_new, kpe_new, wukp, wuv, *([cache_ckv] * pages), *([cache_kpe_t] * pages))


def _gelu_tanh(x):
    return x * (0.5 * (1.0 + jnp.tanh(0.7978845608028654 * (x + 0.044715 * (x * x * x)))))


def _lru_kernel(u_ref, gate_ref, cbuf_ref, h0_ref, cw_ref, cb_ref, wa_ref, ba_ref, wx_ref, bx_ref, lam_ref,
                out_ref, hlast_ref, cnew_ref, ext_scr, a_scr, b_scr, h_scr, hc_scr, *, gb, rb):
    j = pl.program_id(1)
    tm, w = gb * rb, LRU_WIDTH
    pre = SUBLANES
    tail = CONV_W - 1

    @pl.when(j == 0)
    def _first():
        ext_scr[:, 0:pre, :] = jnp.zeros((gb, pre, w), F32)
        ext_scr[:, pre - tail:pre, :] = cbuf_ref[...]
        hc_scr[...] = h0_ref[...]

    ext_scr[:, pre:pre + rb, :] = u_ref[...]
    uc = cb_ref[...][None]
    for t in range(CONV_W):
        uc = uc + ext_scr[:, pre - tail + t:pre - tail + t + rb, :] * cw_ref[t:t + 1, :][None]
    new_tail = ext_scr[:, pre + rb - tail:pre + rb, :]
    ext_scr[:, pre - tail:pre, :] = new_tail
    cnew_ref[...] = new_tail

    uc = uc.reshape(tm, w)
    ub = uc.astype(BF16)
    r = _sigmoid(_dot(ub, wa_ref[...]) + ba_ref[...])
    i = _sigmoid(_dot(ub, wx_ref[...]) + bx_ref[...])
    nl = -lam_ref[...]
    softplus = jnp.maximum(nl, 0.0) + jnp.log(1.0 + jnp.exp(-jnp.abs(nl)))
    log_a = (-LRU_C) * r * softplus
    a = jnp.exp(log_a)
    bt = jnp.sqrt(1.0 - jnp.exp(2.0 * log_a)) * (i * uc)

    rin = _imod(lax.broadcasted_iota(jnp.int32, (tm, w), 0), SUBLANES)
    for sft in (1, 2, 4):
        a_sh = pltpu.roll(a, sft, 0)
        b_sh = pltpu.roll(bt, sft, 0)
        m = rin >= sft
        bt = jnp.where(m, a * b_sh + bt, bt)
        a = jnp.where(m, a * a_sh, a)

    if rb == SUBLANES:
        hseq = a.reshape(gb, rb, w) * hc_scr[...] + bt.reshape(gb, rb, w)
        hc_scr[...] = hseq[:, rb - 1:rb, :]
    else:
        a_scr[...] = a
        b_scr[...] = bt

        def body(g, hprev):
            r0 = pl.multiple_of(g * SUBLANES, SUBLANES)
            hh = a_scr[pl.ds(r0, SUBLANES), :] * hprev + b_scr[pl.ds(r0, SUBLANES), :]
            h_scr[pl.ds(r0, SUBLANES), :] = hh
            return hh[SUBLANES - 1:SUBLANES, :]

        hc_scr[0] = lax.fori_loop(0, rb // SUBLANES, body, hc_scr[0])
        hseq = h_scr[...].reshape(gb, rb, w)
    hlast_ref[...] = hc_scr[...]
    out_ref[...] = (hseq * _gelu_tanh(gate_ref[...])).astype(out_ref.dtype)


def _lru(u, gate, conv_buf, h0, wts, tm_target):
    g, r, w = u.shape
    gb, rb = _token_tiles(g, r, tm_target)
    assert gb == 1 or rb == SUBLANES
    tm = gb * rb
    full = lambda a: pl.BlockSpec(a.shape, lambda i, j, _n=a.ndim: (0,) * _n)
    tok = pl.BlockSpec((gb, rb, w), lambda i, j: (i, j, 0))
    per_g = lambda rows: pl.BlockSpec((gb, rows, w), lambda i, j: (i, 0, 0))
    names = ["conv_w", "conv_b", "wa", "ba", "wx", "bx", "lam"]
    return pl.pallas_call(
        functools.partial(_lru_kernel, gb=gb, rb=rb),
        out_shape=[jax.ShapeDtypeStruct((g, r, w), _act_dtype(gb)), jax.ShapeDtypeStruct((g, 1, w), F32),
                   jax.ShapeDtypeStruct((g, CONV_W - 1, w), F32)],
        grid=(g // gb, r // rb),
        in_specs=[tok, tok, per_g(CONV_W - 1), per_g(1)] + [full(wts[n]) for n in names],
        out_specs=[tok, per_g(1), per_g(CONV_W - 1)],
        scratch_shapes=[pltpu.VMEM((gb, SUBLANES + rb, w), F32), pltpu.VMEM((tm, w), F32),
                        pltpu.VMEM((tm, w), F32), pltpu.VMEM((tm, w), F32), pltpu.VMEM((gb, 1, w), F32)],
        compiler_params=_params("arbitrary", "arbitrary"),
        name="lru",
    )(u, gate, conv_buf, h0, *[wts[n] for n in names])


def _mix_out_kernel(*refs, n_in, gb, rb, alpha):
    a_refs, w_refs = refs[:n_in], refs[n_in:2 * n_in]
    x_ref, gt_ref, lng_ref, lnb_ref, sc2_ref, sh2_ref, wr_ref, br_ref, x1_ref, h2_ref, lg_ref = refs[2 * n_in:]
    tm = gb * rb
    mix = None
    for a_ref, w_ref in zip(a_refs, w_refs):
        part = _dot(a_ref[...].reshape(tm, a_ref.shape[-1]).astype(BF16), w_ref[...])
        mix = part if mix is None else mix + part
    y = alpha * x_ref[...] + (1.0 + gt_ref[...]) * mix.reshape(gb, rb, D_MODEL)
    x1 = _layer_norm(y, lng_ref[...], lnb_ref[...])
    x1_ref[...] = x1
    h2 = x1 * (1.0 + sc2_ref[...]) + sh2_ref[...]
    h2_ref[...] = h2.astype(h2_ref.dtype)
    lg_ref[...] = _dot_nt(wr_ref[...], h2.reshape(tm, D_MODEL).astype(BF16)) + br_ref[...]


def _mix_out(acts, ws, x, gt, lng, lnb, sc2, sh2, wr, br, alpha, tm_target):
    g, r, d = x.shape
    gb, rb = _token_tiles(g, r, tm_target)
    tm = gb * rb
    nj = r // rb
    full = lambda a: pl.BlockSpec(a.shape, lambda i, j, _n=a.ndim: (0,) * _n)
    tok = lambda c: pl.BlockSpec((gb, rb, c), lambda i, j: (i, j, 0))
    mod = pl.BlockSpec((gb, 1, d), lambda i, j: (i, 0, 0))
    return pl.pallas_call(
        functools.partial(_mix_out_kernel, n_in=len(acts), gb=gb, rb=rb, alpha=alpha),
        out_shape=[jax.ShapeDtypeStruct((g, r, d), F32), jax.ShapeDtypeStruct((g, r, d), _act_dtype(gb)),
                   jax.ShapeDtypeStruct((ROUTER_ROWS, g * r), F32)],
        grid=(g // gb, nj),
        in_specs=[tok(a.shape[-1]) for a in acts] + [full(w) for w in ws]
        + [tok(d), mod, full(lng), full(lnb), mod, mod, full(wr), full(br)],
        out_specs=[tok(d), tok(d), pl.BlockSpec((ROUTER_ROWS, tm), lambda i, j: (0, i * nj + j))],
        compiler_params=_params("arbitrary", "arbitrary"),
        name="mix_out",
    )(*acts, *ws, x, gt, lng, lnb, sc2, sh2, wr, br)


def _route_kernel(lg_ref, comb_ref):
    tn = lg_ref.shape[1]
    big = 3.0e38
    grow = lax.broadcasted_iota(jnp.int32, (SUBLANES, tn), 0)
    gl = jnp.where(grow < N_GROUPS, lg_ref[0:SUBLANES, :], -big)
    gmax = jnp.max(gl, axis=0, keepdims=True)
    g_idx = jnp.min(jnp.where(gl == gmax, grow, N_GROUPS), axis=0, keepdims=True)
    g_w = 1.0 / jnp.sum(jnp.where(grow < N_GROUPS, jnp.exp(gl - gmax), 0.0), axis=0, keepdims=True)

    el = lg_ref[EXPERT_ROW0:EXPERT_ROW0 + N_EXPERTS, :]
    erow = lax.broadcasted_iota(jnp.int32, (N_EXPERTS, tn), 0)
    ingrp = _idiv(erow, EXP_PER_GROUP) == g_idx
    emax = jnp.max(jnp.where(ingrp, el, -big), axis=0, keepdims=True)
    ex = jnp.where(ingrp, jnp.exp(el - emax), 0.0)
    p = ex / jnp.sum(ex, axis=0, keepdims=True)
    ps = jnp.where(ingrp, p, -1.0)
    p1 = jnp.max(ps, axis=0, keepdims=True)
    i1 = jnp.min(jnp.where(ps == p1, erow, N_EXPERTS), axis=0, keepdims=True)
    ps2 = jnp.where(erow == i1, -1.0, ps)
    p2 = jnp.max(ps2, axis=0, keepdims=True)
    i2 = jnp.min(jnp.where(ps2 == p2, erow, N_EXPERTS), axis=0, keepdims=True)
    tot = p1 + p2
    comb_ref[...] = (jnp.where(erow == i1, g_w * p1 / tot, 0.0)
                     + jnp.where(erow == i2, g_w * p2 / tot, 0.0))


def _route(logits_t):
    n = logits_t.shape[1]
    tn = min(n, 2048)
    return pl.pallas_call(
        _route_kernel,
        out_shape=jax.ShapeDtypeStruct((N_EXPERTS, n), F32),
        grid=(n // tn,),
        in_specs=[pl.BlockSpec((ROUTER_ROWS, tn), lambda i: (0, i))],
        out_specs=pl.BlockSpec((N_EXPERTS, tn), lambda i: (0, i)),
        compiler_params=_params("arbitrary"),
        name="route",
    )(logits_t)


def _moe_kernel(h_ref, comb_ref, w1_ref, w3_ref, w2_ref, x1_ref, gt_ref, lng_ref, lnb_ref, out_ref, acc_scr,
                *, gb, rb, alpha):
    e = pl.program_id(2)
    tm = gb * rb

    @pl.when(e == 0)
    def _init():
        acc_scr[...] = jnp.zeros(acc_scr.shape, F32)

    h = h_ref[...].reshape(tm, D_MODEL).astype(BF16)
    comb = comb_ref[...].reshape(tm, N_EXPERTS)
    lane = lax.broadcasted_iota(jnp.int32, (tm, N_EXPERTS), 1)
    nx = w1_ref.shape[0]
    ces = [jnp.sum(jnp.where(lane == e * nx + x, comb, 0.0), axis=1, keepdims=True) for x in range(nx)]
    rc = min(tm, MOE_ROW_CHUNK)
    for r0 in range(0, tm, rc):
        c = slice(r0, r0 + rc)
        ab = [(_dot(h[c], w1_ref[x]), _dot(h[c], w3_ref[x])) for x in range(nx)]
        hid = [((a * _sigmoid(a)) * b * ces[x][c]).astype(BF16) for x, (a, b) in enumerate(ab)]
        y = None
        for x in range(nx):
            d = _dot(hid[x], w2_ref[x])
            y = d if y is None else y + d
        acc_scr[c, :] += y

    @pl.when(e == pl.num_programs(2) - 1)
    def _fin():
        y = alpha * x1_ref[...] + (1.0 + gt_ref[...]) * acc_scr[...].reshape(gb, rb, D_MODEL)
        out_ref[...] = _layer_norm(y, lng_ref[...], lnb_ref[...])


def _moe(h2, comb, layer, w1, w3, w2, x1, gt, lng, lnb, alpha, tm_target):
    g, r, d = x1.shape
    gb, rb = _token_tiles(g, r, tm_target)
    nx = MOE_EXPERTS_PER_STEP
    assert w1.shape[1] % nx == 0
    tok = lambda c: pl.BlockSpec((gb, rb, c), lambda i, j, e: (i, j, 0))
    full = lambda a: pl.BlockSpec(a.shape, lambda i, j, e, _n=a.ndim: (0,) * _n)
    wspec = lambda a: pl.BlockSpec((None, nx) + a.shape[2:], lambda i, j, e: (layer, e, 0, 0))
    return pl.pallas_call(
        functools.partial(_moe_kernel, gb=gb, rb=rb, alpha=alpha),
        out_shape=jax.ShapeDtypeStruct((g, r, d), F32),
        grid=(g // gb, r // rb, w1.shape[1] // nx),
        in_specs=[tok(d), tok(N_EXPERTS), wspec(w1), wspec(w3), wspec(w2), tok(d),
                  pl.BlockSpec((gb, 1, d), lambda i, j, e: (i, 0, 0)), full(lng), full(lnb)],
        out_specs=tok(d),
        scratch_shapes=[pltpu.VMEM((gb * rb, d), F32)],
        compiler_params=_params("arbitrary", "arbitrary", "arbitrary"),
        name="moe",
    )(h2, comb, w1, w3, w2, x1, gt, lng, lnb)


def _sb_qkv_kernel(x_ref, sc_ref, sh_ref, w_ref, *outs, prompt, gb, rb):
    tm = gb * rb
    h = (x_ref[...] * (1.0 + sc_ref[...]) + sh_ref[...]).reshape(tm, D_MODEL).astype(BF16)
    qkv = _dot(h, w_ref[...])
    q = qkv[:, :D_MODEL] * SB_SCALE
    k = qkv[:, D_MODEL:2 * D_MODEL]
    v = qkv[:, 2 * D_MODEL:]
    if prompt:
        k_ref, v_ref, qh_ref, kh_ref, vh_ref = outs
        for hp in range(SB_HEADS // 2):
            sl = slice(hp * LANES, (hp + 1) * LANES)
            qh_ref[0, hp] = q[:, sl].astype(BF16)
            kh_ref[0, hp] = k[:, sl].astype(BF16)
            vh_ref[0, hp] = v[:, sl].astype(BF16)
    else:
        k_ref, v_ref, q_ref = outs
        q_ref[...] = q.reshape(gb, rb, D_MODEL)
    k_ref[...] = k.reshape(gb, rb, D_MODEL)
    v_ref[...] = v.reshape(gb, rb, D_MODEL)


def _sb_qkv(x, sc, sh, w, prompt, tm_target):
    g, r, d = x.shape
    gb, rb = _token_tiles(g, r, tm_target)
    tok = pl.BlockSpec((gb, rb, d), lambda i, j: (i, j, 0))
    mod = pl.BlockSpec((gb, 1, d), lambda i, j: (i, 0, 0))
    out_shape = [jax.ShapeDtypeStruct((g, r, d), F32)] * 2
    out_specs = [tok, tok]
    if prompt:
        assert gb == 1
        npair = SB_HEADS // 2
        out_shape += [jax.ShapeDtypeStruct((g, npair, r, LANES), BF16)] * 3
        out_specs += [pl.BlockSpec((1, npair, rb, LANES), lambda i, j: (i, 0, j, 0))] * 3
    else:
        out_shape += [jax.ShapeDtypeStruct((g, r, d), F32)]
        out_specs += [tok]
    return pl.pallas_call(
        functools.partial(_sb_qkv_kernel, prompt=prompt, gb=gb, rb=rb),
        out_shape=out_shape, grid=(g // gb, r // rb),
        in_specs=[tok, mod, mod, pl.BlockSpec(w.shape, lambda i, j: (0, 0))],
        out_specs=out_specs,
        compiler_params=_params("arbitrary", "arbitrary"),
        name="sb_qkv_prompt" if prompt else "sb_qkv_sample",
    )(x, sc, sh, w)


def _sb_terms(z, upper, valid):
    lq = jnp.minimum(z, 0.0) - jnp.log(1.0 + jnp.exp(-jnp.abs(z)))
    lk = lq - z
    if valid is not None:
        lk = jnp.where(valid, lk, 0.0)
    hi = lk.astype(BF16)
    lo = (lk - hi.astype(F32)).astype(BF16)
    return lq + (_dot(hi, upper) + _dot(lo, upper)), jnp.sum(lk, axis=-1, keepdims=True)


def _sb_weights(arg, carry, valid):
    w = jnp.exp(arg + _lanes(carry, arg.shape[1]))
    if valid is not None:
        w = jnp.where(valid, w, 0.0)
    return w.astype(BF16)


def _sb_prompt_kernel(qi_ref, kj_ref, q_ref, k_ref, v_ref, up_ref, o_ref, carry_scr, acc_scr, *, tq):
    t = pl.program_id(1)
    qi, kj = qi_ref[t], kj_ref[t]
    npair = SB_HEADS // 2

    @pl.when(kj == qi)
    def _init():
        carry_scr[...] = jnp.zeros(carry_scr.shape, F32)
        acc_scr[...] = jnp.zeros(acc_scr.shape, F32)

    kb = up_ref.shape[0]

    def run(diagonal):
        valid = None
        if diagonal:
            row = lax.broadcasted_iota(jnp.int32, (tq, tq), 0)
            col = lax.broadcasted_iota(jnp.int32, (tq, tq), 1)
            valid = col < row
        lane = lax.broadcasted_iota(jnp.int32, (tq, LANES), 1)
        upper = up_ref[...]

        hsels = [lane < SB_HEAD_DIM, lane >= SB_HEAD_DIM]
        k0s = list(range(tq - kb, -1, -kb))
        vlds = [None if valid is None else valid[:, k0:k0 + kb] for k0 in k0s]

        def body(i, c):
            hps = [i * SB_PAIRS_PER_TRIP + n for n in range(SB_PAIRS_PER_TRIP)]
            zs = []
            for hp in hps:
                qq, kk = q_ref[0, hp].astype(F32), k_ref[0, hp]
                zs.append([_dot_nt(jnp.where(hsel, qq, 0.0).astype(BF16), kk) for hsel in hsels])
            terms = [[[_sb_terms(z[:, k0:k0 + kb], upper, vld) for k0, vld in zip(k0s, vlds)] for z in zp]
                     for zp in zs]
            for hp, tp in zip(hps, terms):
                vv = v_ref[0, hp]
                out = None
                for sub in range(2):
                    carry = carry_scr[hp, sub]
                    pv = None
                    for (arg, total), k0, vld in zip(tp[sub], k0s, vlds):
                        d = _dot(_sb_weights(arg, carry, vld), vv[k0:k0 + kb])
                        pv = d if pv is None else pv + d
                        carry = carry + total
                    carry_scr[hp, sub] = carry
                    out = jnp.where(hsels[sub], pv, 0.0 if out is None else out)
                acc_scr[hp] += out
            return c

        lax.fori_loop(0, npair // SB_PAIRS_PER_TRIP, body, 0)

    @pl.when(kj == qi)
    def _diag():
        run(True)

    @pl.when(kj < qi)
    def _off():
        run(False)

    @pl.when(kj == 0)
    def _fin():
        for hp in range(npair):
            o_ref[0, :, hp * LANES:(hp + 1) * LANES] = acc_scr[hp].astype(BF16)


def _upper(n):
    j = np.arange(n)[:, None]
    s = np.arange(n)[None, :]
    return jnp.asarray((j > s).astype(np.float32), dtype=BF16)


def _sb_prompt(q, k, v, tq):
    b, npair, s, _ = q.shape
    nq = s // tq
    qi, kj = _triangle(nq, descending=True)
    kb = SB_CUMSUM_KEYS if tq % SB_CUMSUM_KEYS == 0 else tq
    up = _upper(kb)
    blk = lambda sel: pl.BlockSpec((1, npair, tq, LANES), sel)
    grid_spec = pltpu.PrefetchScalarGridSpec(
        num_scalar_prefetch=2, grid=(b, qi.shape[0]),
        in_specs=[blk(lambda bb, t, qi_r, kj_r: (bb, 0, qi_r[t], 0)),
                  blk(lambda bb, t, qi_r, kj_r: (bb, 0, kj_r[t], 0)),
                  blk(lambda bb, t, qi_r, kj_r: (bb, 0, kj_r[t], 0)),
                  pl.BlockSpec((kb, kb), lambda bb, t, qi_r, kj_r: (0, 0))],
        out_specs=pl.BlockSpec((1, tq, npair * LANES), lambda bb, t, qi_r, kj_r: (bb, qi_r[t], 0)),
        scratch_shapes=[pltpu.VMEM((npair, 2, tq, LANES), F32), pltpu.VMEM((npair, tq, LANES), F32)],
    )
    return pl.pallas_call(
        functools.partial(_sb_prompt_kernel, tq=tq),
        out_shape=jax.ShapeDtypeStruct((b, s, npair * LANES), BF16),
        grid_spec=grid_spec,
        compiler_params=_params("arbitrary", "arbitrary"),
        name="sb_prompt",
    )(qi, kj, q, k, v, up)


def _sb_sample_kernel(pt_ref, q_ref, kn_ref, vn_ref, up_ref, *rest, pages, rows):
    k_refs, v_refs = rest[:pages], rest[pages:2 * pages]
    o_ref, qbd_scr, carry_scr, acc_scr = rest[2 * pages:]
    j = pl.program_id(1)
    nrow = SB_HEADS * rows
    upper = up_ref[...]

    @pl.when(j == 0)
    def _first():
        qt = jnp.concatenate([q_ref[0]] * SB_HEADS, axis=0)
        rgrp = _idiv(lax.broadcasted_iota(jnp.int32, (nrow, D_MODEL), 0), rows)
        lgrp = _idiv(lax.broadcasted_iota(jnp.int32, (nrow, D_MODEL), 1), SB_HEAD_DIM)
        qbd = jnp.where(rgrp == lgrp, qt, 0.0).astype(BF16)
        qbd_scr[...] = qbd
        pad = jnp.zeros((PAGE_SIZE - rows, D_MODEL), F32)
        kn = jnp.concatenate([kn_ref[0], pad], axis=0).astype(BF16)
        vn = jnp.concatenate([vn_ref[0], pad], axis=0).astype(BF16)
        row = lax.broadcasted_iota(jnp.int32, (nrow, PAGE_SIZE), 0)
        col = lax.broadcasted_iota(jnp.int32, (nrow, PAGE_SIZE), 1)
        valid = col < _imod(row, rows)
        arg, total = _sb_terms(_dot_nt(qbd, kn), upper[:PAGE_SIZE, :PAGE_SIZE], valid)
        w = _sb_weights(arg, jnp.zeros((nrow, LANES), F32), valid)
        carry_scr[...] = jnp.broadcast_to(total, (nrow, LANES))
        acc_scr[...] = _dot(w, vn)

    qbd = qbd_scr[...]
    pairs = range(0, pages, 2)
    zs = [_dot(qbd, jnp.concatenate([k_refs[i + 1][0, 0], k_refs[i][0, 0]], axis=1).astype(BF16))
          for i in pairs]
    terms = [_sb_terms(z, upper, None) for z in zs]
    carry = carry_scr[...]
    pv = None
    for i, (arg, total) in zip(pairs, terms):
        vt = jnp.concatenate([v_refs[i + 1][0, 0], v_refs[i][0, 0]], axis=1).astype(BF16)
        d = _dot_nt(_sb_weights(arg, carry, None), vt)
        pv = d if pv is None else pv + d
        carry = carry + total
    carry_scr[...] = carry
    acc_scr[...] += pv

    @pl.when(j == pl.num_programs(1) - 1)
    def _last():
        lgrp = _idiv(lax.broadcasted_iota(jnp.int32, (rows, D_MODEL), 1), SB_HEAD_DIM)
        out = jnp.zeros((rows, D_MODEL), F32)
        for hd in range(SB_HEADS):
            out = out + jnp.where(lgrp == hd, acc_scr[hd * rows:(hd + 1) * rows, :], 0.0)
        o_ref[0] = out


def _sb_sample(q, k_new, v_new, cache_k, cache_v, o_idx, page_table, pages):
    b, rows, d = q.shape
    n_pages = page_table.shape[1]
    assert n_pages % pages == 0 and pages % 2 == 0 and rows == SUBLANES
    steps = n_pages // pages
    pt = page_table.reshape(-1)
    nrow = SB_HEADS * rows
    up = _upper(2 * PAGE_SIZE)

    def page_spec(i):
        return pl.BlockSpec(
            (1, 1, d, PAGE_SIZE),
            lambda bb, j, pt_r, _i=i: (o_idx, pt_r[bb * n_pages + n_pages - 1 - (j * pages + _i)], 0, 0))

    seq = pl.BlockSpec((1, rows, d), lambda bb, j, pt_r: (bb, 0, 0))
    grid_spec = pltpu.PrefetchScalarGridSpec(
        num_scalar_prefetch=1, grid=(b, steps),
        in_specs=[seq, seq, seq, pl.BlockSpec(up.shape, lambda bb, j, pt_r: (0, 0))]
        + [page_spec(i) for i in range(pages)] * 2,
        out_specs=seq,
        scratch_shapes=[pltpu.VMEM((nrow, d), BF16), pltpu.VMEM((nrow, LANES), F32), pltpu.VMEM((nrow, d), F32)],
    )
    return pl.pallas_call(
        functools.partial(_sb_sample_kernel, pages=pages, rows=rows),
        out_shape=jax.ShapeDtypeStruct((b, rows, d), F32),
        grid_spec=grid_spec,
        compiler_params=_params("arbitrary", "arbitrary"),
        name="sb_sample",
    )(pt, q, k_new, v_new, up, *([cache_k] * pages), *([cache_v] * pages))


def _prep_layer0(e, w_in_ab, g_q, w_uq, g_kv, w_uk, w_uv, conv_w, conv_b, w_rg_a, b_rg_a, w_rg_x, b_rg_x,
                 lru_lambda):
    half = QK_ROPE // 2
    w_in = w_in_ab[e]
    o1, o2, o3, o4 = Q_LORA, Q_LORA + KV_LORA, Q_LORA + KV_LORA + QK_ROPE, Q_LORA + KV_LORA + QK_ROPE + LRU_WIDTH
    kpe_w = w_in[:, o2:o3]
    zpad = jnp.zeros((D_MODEL, LANES - QK_ROPE), F32)
    win = jnp.concatenate([
        w_in[:, :o2], w_in[:, o3:o4], w_in[:, o4:],
        kpe_w, zpad,
        kpe_w[:, half:], kpe_w[:, :half], zpad], axis=1).astype(BF16)

    wq = w_uq[e].reshape(Q_LORA, MLA_HEADS, QK_NOPE + QK_ROPE)
    nope, x1, x2 = wq[..., :QK_NOPE], wq[..., QK_NOPE:QK_NOPE + half], wq[..., QK_NOPE + half:]
    z32 = jnp.zeros((Q_LORA, MLA_HEADS, HEAD_LANES - QK_NOPE - QK_ROPE), F32)
    wqa = jnp.concatenate([x1, x2, z32, nope], axis=-1).reshape(Q_LORA, MLA_HEADS * HEAD_LANES).astype(BF16)
    wqb = jnp.concatenate([x2, x1, z32, jnp.zeros_like(nope)], axis=-1)
    wqb = wqb.reshape(Q_LORA, MLA_HEADS * HEAD_LANES).astype(BF16)

    wk_top = jnp.concatenate([jnp.zeros((KV_LORA, MLA_HEADS, HEAD_LANES - QK_NOPE), F32), w_uk[e]], axis=-1)
    eye = jnp.eye(LANES, HEAD_LANES, dtype=F32) * (jnp.arange(LANES) < QK_ROPE)[:, None]
    wk_bot = jnp.broadcast_to(eye[:, None, :], (LANES, MLA_HEADS, HEAD_LANES))
    wkp = jnp.concatenate([wk_top, wk_bot], axis=0).reshape(KV_LORA + LANES, MLA_HEADS * HEAD_LANES).astype(BF16)
    wv = w_uv[e].reshape(KV_LORA, MLA_HEADS * V_HEAD).astype(BF16)
    wukp = jnp.concatenate([jnp.zeros((MLA_HEADS, HEAD_LANES - QK_NOPE, KV_LORA), F32),
                            jnp.transpose(w_uk[e], (1, 2, 0))], axis=1).astype(BF16)

    def block_diag(wb):
        n, k, _ = wb.shape
        eye_n = jnp.eye(n, dtype=F32)
        return (wb[:, :, None, :] * eye_n[:, None, :, None]).reshape(n * k, n * k).astype(BF16)

    row = lambda v: v.reshape(1, -1)
    return dict(
        win=win, gq=row(g_q[e]), gkv=row(g_kv[e]), wqa=wqa, wqb=wqb, wkp=wkp, wv=wv, wukp=wukp,
        conv_w=conv_w[e], conv_b=row(conv_b[e]), wa=block_diag(w_rg_a[e]), ba=row(b_rg_a[e]),
        wx=block_diag(w_rg_x[e]), bx=row(b_rg_x[e]), lam=row(lru_lambda[e]))


def _rope_tables(pos):
    half = QK_ROPE // 2
    inv = ROPE_THETA ** (-jnp.arange(half, dtype=F32) / half)
    ang = pos.astype(F32)[:, None] * inv
    cos, sin = jnp.cos(ang), jnp.sin(ang)
    n = pos.shape[0]
    z32 = jnp.zeros((n, HEAD_LANES - QK_NOPE - QK_ROPE), F32)
    ctab = jnp.concatenate([cos, cos, z32, jnp.ones((n, QK_NOPE), F32)], axis=1)
    stab = jnp.concatenate([-sin, sin, z32, jnp.zeros((n, QK_NOPE), F32)], axis=1)
    return ctab, stab


def _router_weights(w_rg, b_rg, w_re, b_re):
    wr = jnp.zeros((ROUTER_ROWS, D_MODEL), F32)
    wr = wr.at[:N_GROUPS].set(w_rg.T).at[EXPERT_ROW0:EXPERT_ROW0 + N_EXPERTS].set(w_re.T)
    br = jnp.zeros((ROUTER_ROWS, 1), F32)
    br = br.at[:N_GROUPS, 0].set(b_rg).at[EXPERT_ROW0:EXPERT_ROW0 + N_EXPERTS, 0].set(b_re)
    return wr.astype(BF16), br


def _run_group(x, c, pos, past, prep, tiles):
    depth = len(prep["layers"])
    alpha = (2.0 * depth) ** 0.25
    prompt = past is None
    g, r, d = x.shape
    mods = _adaln(c, prep["w_mod"], prep["b_mod"])
    ctab, stab = _rope_tables(pos)
    ckv_l, kpe_l, h_l, conv_l, k_l, v_l = [], [], [], [], [], []
    for l in range(depth):
        sh1, sc1, gt1, sh2, sc2, gt2 = mods[l]
        lw = prep["layers"][l]
        if l % 2 == 0:
            e = l // 2
            w0 = lw["mixer"]
            if prompt:
                q, kf, vv, ckv, kpe, u, gate = _ab_in(x, sc1, sh1, w0, ctab, stab, True, tiles["tok"])
                attn = _mla_prompt(q, kf, vv, tiles["attn"])
                h0 = jnp.zeros((g, 1, LRU_WIDTH), F32)
                conv_buf = jnp.zeros((g, CONV_W - 1, LRU_WIDTH), F32)
            else:
                cache_ckv, cache_kpe, st_h, st_conv, _, _, page_table = past
                q, ckv, kpe, u, gate = _ab_in(x, sc1, sh1, w0, ctab, stab, False, tiles["tok"])
                attn = _mla_sample(q, ckv, kpe, cache_ckv, cache_kpe, e, page_table, w0["wukp"], w0["wv"],
                                   tiles["mla_pages"])
                h0 = st_h[e][:, None, :]
                conv_buf = st_conv[e]
            lru_out, h_last, conv_new = _lru(u, gate, conv_buf, h0, w0, tiles["lru"])
            acts, ws = [attn, lru_out], [lw["w_out_a"], lw["w_out_b"]]
            ckv_l.append(ckv)
            kpe_l.append(kpe)
            h_l.append(h_last[:, 0, :])
            conv_l.append(conv_new)
        else:
            o = l // 2
            if prompt:
                k, v, qh, kh, vh = _sb_qkv(x, sc1, sh1, lw["w_qkv"], True, tiles["tok"])
                att = _sb_prompt(qh, kh, vh, tiles["attn"])
            else:
                k, v, q = _sb_qkv(x, sc1, sh1, lw["w_qkv"], False, tiles["tok"])
                att = _sb_sample(q, k, v, past[4], past[5], o, past[6], tiles["sb_pages"])
            acts, ws = [att], [lw["w_out"]]
            k_l.append(k.reshape(g, r, SB_HEADS, SB_HEAD_DIM))
            v_l.append(v.reshape(g, r, SB_HEADS, SB_HEAD_DIM))
        x1, h2, logits_t = _mix_out(acts, ws, x, gt1, lw["ln_g1"], lw["ln_b1"], sc2, sh2, lw["wr"], lw["br"],
                                    alpha, tiles["tok"])
        comb = _route(logits_t).T.reshape(g, r, N_EXPERTS)
        x = _moe(h2, comb, l, prep["w1"], prep["w3"], prep["w2"], x1, gt2, lw["ln_g2"], lw["ln_b2"], alpha,
                 tiles["moe"])
    return x, (jnp.stack(ckv_l), jnp.stack(kpe_l), jnp.stack(h_l), jnp.stack(conv_l), jnp.stack(k_l),
               jnp.stack(v_l))


def kernel(x_prompt, x_sample, cache_mla_ckv, cache_mla_kpe, cache_sb_k, cache_sb_v, state_lru_h, state_conv,
           page_table, c_prompt, c_sample, w_mod, b_mod, ln_g, ln_b, w_in_ab, g_q, w_uq, g_kv, w_uk, w_uv,
           conv_w, conv_b, w_rg_a, b_rg_a, w_rg_x, b_rg_x, lru_lambda, w_out_ab, w_qkv_c, w_out_c,
           w_router_g, b_router_g, w_router_e, b_router_e, w_e1, w_e3, w_e2):
    depth = w_mod.shape[0]
    row = lambda v: v.reshape(1, -1)
    layers = []
    for l in range(depth):
        wr, br = _router_weights(w_router_g[l], b_router_g[l], w_router_e[l], b_router_e[l])
        lw = dict(ln_g1=row(ln_g[l, 0]), ln_b1=row(ln_b[l, 0]), ln_g2=row(ln_g[l, 1]), ln_b2=row(ln_b[l, 1]),
                  wr=wr, br=br)
        if l % 2 == 0:
            e = l // 2
            lw["mixer"] = _prep_layer0(e, w_in_ab, g_q, w_uq, g_kv, w_uk, w_uv, conv_w, conv_b, w_rg_a, b_rg_a,
                                       w_rg_x, b_rg_x, lru_lambda)
            wo = w_out_ab[e].astype(BF16)
            lw["w_out_a"], lw["w_out_b"] = wo[:MLA_HEADS * V_HEAD], wo[MLA_HEADS * V_HEAD:]
        else:
            o = l // 2
            lw["w_qkv"] = w_qkv_c[o].astype(BF16)
            lw["w_out"] = w_out_c[o].astype(BF16)
        layers.append(lw)
    prep = dict(w_mod=w_mod, b_mod=b_mod, layers=layers,
                w1=w_e1.astype(BF16), w3=w_e3.astype(BF16), w2=w_e2.astype(BF16))

    n_pool = cache_sb_k.shape[1]
    cache_k = jnp.transpose(cache_sb_k, (0, 1, 3, 4, 2)).reshape(cache_sb_k.shape[0], n_pool, D_MODEL, PAGE_SIZE)
    cache_v = jnp.transpose(cache_sb_v, (0, 1, 3, 4, 2)).reshape(cache_sb_v.shape[0], n_pool, D_MODEL, PAGE_SIZE)
    n_pages = page_table.shape[1]
    past = (cache_mla_ckv, cache_mla_kpe, state_lru_h, state_conv, cache_k, cache_v, page_table)

    seq = x_prompt.shape[1]
    tiles_p = dict(tok=min(256, seq), attn=min(512, seq), lru=min(512, seq), moe=min(1024, seq))
    n_s = x_sample.shape[0] * x_sample.shape[1]
    tiles_s = dict(tok=min(256, n_s), lru=n_s, moe=min(1024, n_s),
                   mla_pages=min(64, n_pages), sb_pages=min(8, n_pages))

    pos_p = jnp.arange(seq, dtype=jnp.int32)
    pos_s = n_pages * PAGE_SIZE + jnp.arange(x_sample.shape[1], dtype=jnp.int32)
    y_p, (ckv_p, kpe_p, h_p, conv_p, k_p, v_p) = _run_group(x_prompt, c_prompt, pos_p, None, prep, tiles_p)
    y_s, (ckv_s, kpe_s, h_s, conv_s, k_s, v_s) = _run_group(x_sample, c_sample, pos_s, past, prep, tiles_s)
    return (y_p, y_s, ckv_p, kpe_p, h_p, conv_p, k_p, v_p, ckv_s, kpe_s, h_s, conv_s, k_s, v_s)
```

```python
import functools

import numpy as np
import jax
import jax.numpy as jnp
from jax import lax
from jax.experimental import pallas as pl
from jax.experimental.pallas import tpu as pltpu

F32, BF16 = jnp.float32, jnp.bfloat16

D_MODEL = 1024
MLA_HEADS, QK_NOPE, QK_ROPE, V_HEAD = 8, 64, 32, 64
Q_LORA, KV_LORA = 384, 256
ROPE_THETA = 10000.0
MLA_SCALE = (QK_NOPE + QK_ROPE) ** -0.5
LRU_WIDTH, LRU_BLOCKS, CONV_W, LRU_C = 512, 8, 4, 8.0
SB_HEADS, SB_HEAD_DIM = 16, 64
SB_SCALE = SB_HEAD_DIM ** -0.5
N_GROUPS, EXP_PER_GROUP, N_EXPERTS, D_EXPERT = 4, 4, 16, 512
LN_EPS, RMS_EPS = 1e-5, 1e-6
NEG_INF = -1e30
PAGE_SIZE = 128

LANES = 128
SUBLANES = 8
VMEM_LIMIT_BYTES = 56 * 1024 * 1024

MXU_DIM = 256
SB_CUMSUM_KEYS = MXU_DIM
MOE_ROW_CHUNK = MXU_DIM
MOE_EXPERTS_PER_STEP = 4
SB_PAIRS_PER_TRIP = 2
MLA_PAGES_PER_GROUP = 16

HEAD_LANES = LANES
ROUTER_ROWS = 128
EXPERT_ROW0 = 8


def _dot(a, b):
    return jnp.dot(a, b, preferred_element_type=F32)


def _dot_nt(a, b):
    return lax.dot_general(a, b, (((1,), (1,)), ((), ())), preferred_element_type=F32)


def _sigmoid(x):
    return 1.0 / (1.0 + jnp.exp(-x))


def _layer_norm(y, g, b):
    mu = jnp.mean(y, axis=-1, keepdims=True)
    d = y - mu
    var = jnp.mean(d * d, axis=-1, keepdims=True)
    return d * lax.rsqrt(var + LN_EPS) * g + b


def _rms_norm(x, g):
    return x * lax.rsqrt(jnp.mean(x * x, axis=-1, keepdims=True) + RMS_EPS) * g


def _shift_of(n):
    s = int(n).bit_length() - 1
    assert 1 << s == n
    return s


def _idiv(x, n):
    return lax.shift_right_logical(x, _shift_of(n))


def _imod(x, n):
    assert 1 << _shift_of(n) == n
    return x & (n - 1)


def _act_dtype(gb):
    return BF16 if gb == 1 else F32


def _lanes(x, width):
    n = width // LANES
    return x if n == 1 else jnp.concatenate([x] * n, axis=1)


def _params(*sem):
    return pltpu.CompilerParams(dimension_semantics=sem, vmem_limit_bytes=VMEM_LIMIT_BYTES)


def _token_tiles(groups, rows, target):
    if rows >= target:
        assert rows % target == 0
        return 1, target
    gb = min(groups, max(1, target // rows))
    assert groups % gb == 0
    return gb, rows


def _adaln_kernel(c_ref, w_ref, b_ref, o_ref):
    c = c_ref[...]
    s = (c * _sigmoid(c)).astype(BF16)
    o_ref[0] = _dot(s, w_ref[0].astype(BF16)) + b_ref[0]


def _adaln(c, w_mod, b_mod):
    depth, d, d6 = w_mod.shape
    b = c.shape[0]
    bp = -(-b // SUBLANES) * SUBLANES
    cp = jnp.pad(c, ((0, bp - b), (0, 0)))
    tn = 1536
    out = pl.pallas_call(
        _adaln_kernel,
        out_shape=jax.ShapeDtypeStruct((depth, bp, d6), F32),
        grid=(depth, d6 // tn),
        in_specs=[
            pl.BlockSpec((bp, d), lambda l, n: (0, 0)),
            pl.BlockSpec((1, d, tn), lambda l, n: (l, 0, n)),
            pl.BlockSpec((1, 1, tn), lambda l, n: (l, 0, n)),
        ],
        out_specs=pl.BlockSpec((1, bp, tn), lambda l, n: (l, 0, n)),
        compiler_params=_params("arbitrary", "arbitrary"),
        name="adaln",
    )(cp, w_mod, b_mod.reshape(depth, 1, d6))
    m = out[:, :b].reshape(depth, b, 6, 1, d)
    return [[m[l, :, i] for i in range(6)] for l in range(depth)]


def _ab_in_kernel(x_ref, sc_ref, sh_ref, win_ref, gq_ref, gkv_ref, wqa_ref, wqb_ref, ct_ref, st_ref,
                  *rest, prompt, gb, rb):
    if prompt:
        wkp_ref, wv_ref, q_ref, kf_ref, v_ref, ckv_ref, kpe_ref, u_ref, gate_ref = rest
    else:
        q_ref, ckv_ref, kpe_ref, u_ref, gate_ref = rest
    tm = gb * rb
    h = (x_ref[...] * (1.0 + sc_ref[...]) + sh_ref[...]).reshape(tm, D_MODEL).astype(BF16)
    p = _dot(h, win_ref[...])
    o_kv, o_u, o_g, o_ka, o_kb = Q_LORA, Q_LORA + KV_LORA, Q_LORA + KV_LORA + LRU_WIDTH, \
        Q_LORA + KV_LORA + 2 * LRU_WIDTH, Q_LORA + KV_LORA + 2 * LRU_WIDTH + LANES
    q_lat, kv_lat = p[:, :o_kv], p[:, o_kv:o_u]
    u_ref[...] = p[:, o_u:o_g].reshape(gb, rb, LRU_WIDTH)
    gate_ref[...] = p[:, o_g:o_ka].reshape(gb, rb, LRU_WIDTH)
    kpe_a, kpe_b = p[:, o_ka:o_kb], p[:, o_kb:o_kb + LANES]

    c = jnp.broadcast_to(ct_ref[...][None], (gb, rb, LANES)).reshape(tm, LANES)
    s = jnp.broadcast_to(st_ref[...][None], (gb, rb, LANES)).reshape(tm, LANES)

    qn = _rms_norm(q_lat, gq_ref[...]).astype(BF16)
    qa = _dot(qn, wqa_ref[...])
    qb = _dot(qn, wqb_ref[...])
    ckv = _rms_norm(kv_lat, gkv_ref[...])
    ckv_ref[...] = ckv.reshape(gb, rb, KV_LORA)
    kpe = kpe_a * c + kpe_b * s
    kpe_ref[...] = kpe[:, :QK_ROPE].reshape(gb, rb, QK_ROPE)

    for hd in range(MLA_HEADS):
        sl = slice(hd * HEAD_LANES, (hd + 1) * HEAD_LANES)
        qh = qa[:, sl] * c + qb[:, sl] * s
        if prompt:
            q_ref[0, hd] = qh.astype(BF16)
        else:
            q_ref[:, :, sl] = qh.reshape(gb, rb, HEAD_LANES)
    if prompt:
        ckv_bf = ckv.astype(BF16)
        kin = jnp.concatenate([ckv_bf, kpe.astype(BF16)], axis=-1)
        kf = _dot(kin, wkp_ref[...])
        vv = _dot(ckv_bf, wv_ref[...])
        for hd in range(MLA_HEADS):
            kf_ref[0, hd] = kf[:, hd * HEAD_LANES:(hd + 1) * HEAD_LANES].astype(BF16)
        for hp in range(MLA_HEADS // 2):
            v_ref[0, hp] = vv[:, hp * LANES:(hp + 1) * LANES].astype(BF16)


def _ab_in(x, sc, sh, wts, ctab, stab, prompt, tm_target):
    g, r, d = x.shape
    gb, rb = _token_tiles(g, r, tm_target)
    grid = (g // gb, r // rb)
    full = lambda a: pl.BlockSpec(a.shape, lambda i, j, _n=a.ndim: (0,) * _n)
    tok = lambda c: pl.BlockSpec((gb, rb, c), lambda i, j: (i, j, 0))
    mod = pl.BlockSpec((gb, 1, d), lambda i, j: (i, 0, 0))
    tab = pl.BlockSpec((rb, LANES), lambda i, j: (j, 0))
    ins = [x, sc, sh, wts["win"], wts["gq"], wts["gkv"], wts["wqa"], wts["wqb"], ctab, stab]
    in_specs = [tok(d), mod, mod, full(wts["win"]), full(wts["gq"]), full(wts["gkv"]),
                full(wts["wqa"]), full(wts["wqb"]), tab, tab]
    tail_shapes = [jax.ShapeDtypeStruct((g, r, KV_LORA), F32), jax.ShapeDtypeStruct((g, r, QK_ROPE), F32),
                   jax.ShapeDtypeStruct((g, r, LRU_WIDTH), F32), jax.ShapeDtypeStruct((g, r, LRU_WIDTH), F32)]
    tail_specs = [tok(KV_LORA), tok(QK_ROPE), tok(LRU_WIDTH), tok(LRU_WIDTH)]
    if prompt:
        assert gb == 1
        ins += [wts["wkp"], wts["wv"]]
        in_specs += [full(wts["wkp"]), full(wts["wv"])]
        hm = lambda nh: pl.BlockSpec((1, nh, rb, LANES), lambda i, j: (i, 0, j, 0))
        out_shape = [jax.ShapeDtypeStruct((g, MLA_HEADS, r, LANES), BF16),
                     jax.ShapeDtypeStruct((g, MLA_HEADS, r, LANES), BF16),
                     jax.ShapeDtypeStruct((g, MLA_HEADS // 2, r, LANES), BF16)] + tail_shapes
        out_specs = [hm(MLA_HEADS), hm(MLA_HEADS), hm(MLA_HEADS // 2)] + tail_specs
    else:
        out_shape = [jax.ShapeDtypeStruct((g, r, MLA_HEADS * HEAD_LANES), F32)] + tail_shapes
        out_specs = [tok(MLA_HEADS * HEAD_LANES)] + tail_specs
    return pl.pallas_call(
        functools.partial(_ab_in_kernel, prompt=prompt, gb=gb, rb=rb),
        out_shape=out_shape, grid=grid, in_specs=in_specs, out_specs=out_specs,
        compiler_params=_params("arbitrary", "arbitrary"),
        name="ab_in_prompt" if prompt else "ab_in_sample",
    )(*ins)


def _triangle(nq, descending):
    qi, kj = [], []
    for q in range(nq):
        ks = range(q, -1, -1) if descending else range(q + 1)
        for k in ks:
            qi.append(q)
            kj.append(k)
    return jnp.asarray(np.array(qi, np.int32)), jnp.asarray(np.array(kj, np.int32))


def _mla_prompt_kernel(qi_ref, kj_ref, q_ref, k_ref, v_ref, o_ref, m_scr, l_scr, acc_scr, *, tq):
    t = pl.program_id(1)
    qi, kj = qi_ref[t], kj_ref[t]

    @pl.when(kj == 0)
    def _init():
        m_scr[...] = jnp.full(m_scr.shape, NEG_INF, F32)
        l_scr[...] = jnp.zeros(l_scr.shape, F32)
        acc_scr[...] = jnp.zeros(acc_scr.shape, F32)

    def run(diagonal):
        if diagonal:
            row = lax.broadcasted_iota(jnp.int32, (tq, tq), 0)
            col = lax.broadcasted_iota(jnp.int32, (tq, tq), 1)
            valid = col <= row

        def body(i, carry):
            hds = [4 * i + n for n in range(4)]
            ss = [_dot_nt(q_ref[0, hd], k_ref[0, hd]) * MLA_SCALE for hd in hds]
            if diagonal:
                ss = [jnp.where(valid, s, NEG_INF) for s in ss]
            ps, alphas = [], []
            for hd, s in zip(hds, ss):
                m_prev = m_scr[hd]
                m_new = jnp.maximum(m_prev, jnp.max(s, axis=-1, keepdims=True))
                alpha = jnp.exp(m_prev - m_new)
                p = jnp.exp(s - _lanes(m_new, tq))
                l_scr[hd] = alpha * l_scr[hd] + jnp.sum(p, axis=-1, keepdims=True)
                m_scr[hd] = m_new
                ps.append(p.astype(BF16))
                alphas.append(alpha)
            for n, (hd, p, alpha) in enumerate(zip(hds, ps, alphas)):
                acc_scr[hd] = alpha * acc_scr[hd] + _dot(p, v_ref[0, 2 * i + n // 2])
            return carry

        lax.fori_loop(0, MLA_HEADS // 4, body, 0)

    @pl.when(kj < qi)
    def _off():
        run(False)

    @pl.when(kj == qi)
    def _diag():
        run(True)
        lane = lax.broadcasted_iota(jnp.int32, (tq, LANES), 1)
        for hp in range(MLA_HEADS // 2):
            o0 = acc_scr[2 * hp] / l_scr[2 * hp]
            o1 = acc_scr[2 * hp + 1] / l_scr[2 * hp + 1]
            o_ref[0, :, hp * LANES:(hp + 1) * LANES] = jnp.where(lane < V_HEAD, o0, o1).astype(BF16)


def _mla_prompt(q, k, v, tq):
    b, nh, s, _ = q.shape
    nq = s // tq
    qi, kj = _triangle(nq, descending=False)
    grid_spec = pltpu.PrefetchScalarGridSpec(
        num_scalar_prefetch=2, grid=(b, qi.shape[0]),
        in_specs=[
            pl.BlockSpec((1, nh, tq, LANES), lambda bb, t, qi_r, kj_r: (bb, 0, qi_r[t], 0)),
            pl.BlockSpec((1, nh, tq, LANES), lambda bb, t, qi_r, kj_r: (bb, 0, kj_r[t], 0)),
            pl.BlockSpec((1, nh // 2, tq, LANES), lambda bb, t, qi_r, kj_r: (bb, 0, kj_r[t], 0)),
        ],
        out_specs=pl.BlockSpec((1, tq, nh * V_HEAD), lambda bb, t, qi_r, kj_r: (bb, qi_r[t], 0)),
        scratch_shapes=[pltpu.VMEM((nh, tq, LANES), F32), pltpu.VMEM((nh, tq, LANES), F32),
                        pltpu.VMEM((nh, tq, LANES), F32)],
    )
    return pl.pallas_call(
        functools.partial(_mla_prompt_kernel, tq=tq),
        out_shape=jax.ShapeDtypeStruct((b, s, nh * V_HEAD), BF16),
        grid_spec=grid_spec,
        compiler_params=_params("arbitrary", "arbitrary"),
        name="mla_prompt",
    )(qi, kj, q, k, v)


def _mla_sample_kernel(pt_ref, q_ref, cn_ref, kn_ref, wuk_ref, wuv_ref, *rest, pages, rows):
    c_refs, k_refs = rest[:pages], rest[pages:2 * pages]
    o_ref, qabs_scr, qpe_scr, m_scr, l_scr, acc_scr = rest[2 * pages:]
    j = pl.program_id(1)
    nrow = MLA_HEADS * rows

    def softmax_step(state, scores, values):
        m_prev, l_prev, acc_prev = state
        smax = scores[0]
        for s in scores[1:]:
            smax = jnp.maximum(smax, s)
        m_new = jnp.maximum(m_prev, jnp.max(smax, axis=-1, keepdims=True))
        alpha = jnp.exp(m_prev - m_new)
        ps = [jnp.exp(s - _lanes(m_new, s.shape[1])) for s in scores]
        psum = ps[0]
        for p in ps[1:]:
            psum = psum + p
        pv = None
        for p, c in zip(ps, values):
            d = _dot(p.astype(BF16), c)
            pv = d if pv is None else pv + d
        return (m_new, alpha * l_prev + jnp.sum(psum, axis=-1, keepdims=True),
                _lanes(alpha, KV_LORA) * acc_prev + pv)

    def load_state():
        return m_scr[...], l_scr[...], acc_scr[...]

    def store_state(state):
        m_scr[...], l_scr[...], acc_scr[...] = state

    @pl.when(j == 0)
    def _first():
        q = q_ref[0]
        q64 = jnp.concatenate([q[:, hd * HEAD_LANES:(hd + 1) * HEAD_LANES] for hd in range(MLA_HEADS)],
                              axis=0)
        q64_bf = q64.astype(BF16)
        rgrp = _idiv(lax.broadcasted_iota(jnp.int32, (nrow, KV_LORA), 0), rows)
        qabs = jnp.zeros((nrow, KV_LORA), F32)
        for hd in range(MLA_HEADS):
            qabs = qabs + jnp.where(rgrp == hd, _dot(q64_bf, wuk_ref[hd]), 0.0)
        qabs_scr[...] = qabs.astype(BF16)
        qpe_scr[...] = q64_bf[:, :QK_ROPE]
        m_scr[...] = jnp.full(m_scr.shape, NEG_INF, F32)
        l_scr[...] = jnp.zeros(l_scr.shape, F32)
        acc_scr[...] = jnp.zeros(acc_scr.shape, F32)
        pad = PAGE_SIZE - rows
        cn = jnp.concatenate([cn_ref[0], jnp.zeros((pad, KV_LORA), F32)], axis=0).astype(BF16)
        kn = jnp.concatenate([kn_ref[0], jnp.zeros((pad, QK_ROPE), F32)], axis=0).astype(BF16)
        row = lax.broadcasted_iota(jnp.int32, (nrow, PAGE_SIZE), 0)
        col = lax.broadcasted_iota(jnp.int32, (nrow, PAGE_SIZE), 1)
        s = (_dot_nt(qabs_scr[...], cn) + _dot_nt(qpe_scr[...], kn)) * MLA_SCALE
        store_state(softmax_step(load_state(), [jnp.where(col <= _imod(row, rows), s, NEG_INF)], [cn]))

    qabs, qpe = qabs_scr[...], qpe_scr[...]

    def group_scores(first_page):
        idx = range(first_page, min(first_page + MLA_PAGES_PER_GROUP, pages), 2)
        cs = [jnp.concatenate([c_refs[i][0, 0], c_refs[i + 1][0, 0]], axis=0).astype(BF16) for i in idx]
        ks = [jnp.concatenate([k_refs[i][0, 0], k_refs[i + 1][0, 0]], axis=1).astype(BF16) for i in idx]
        return [(_dot_nt(qabs, c) + _dot(qpe, k)) * MLA_SCALE for c, k in zip(cs, ks)], cs

    firsts = list(range(0, pages, MLA_PAGES_PER_GROUP))
    state = load_state()
    nxt = group_scores(firsts[0])
    for gi in range(len(firsts)):
        cur = nxt
        if gi + 1 < len(firsts):
            nxt = group_scores(firsts[gi + 1])
        state = softmax_step(state, *cur)
    store_state(state)

    @pl.when(j == pl.num_programs(1) - 1)
    def _last():
        o_lat = (acc_scr[...] / _lanes(l_scr[...], KV_LORA)).astype(BF16)
        o_full = _dot(o_lat, wuv_ref[...])
        lane_h = _idiv(lax.broadcasted_iota(jnp.int32, (rows, MLA_HEADS * V_HEAD), 1), V_HEAD)
        out = jnp.zeros((rows, MLA_HEADS * V_HEAD), F32)
        for hd in range(MLA_HEADS):
            out = out + jnp.where(lane_h == hd, o_full[hd * rows:(hd + 1) * rows], 0.0)
        o_ref[0] = out


def _mla_sample(q, ckv_new, kpe_new, cache_ckv, cache_kpe, e, page_table, wukp, wuv, pages):
    b, rows, _ = q.shape
    n_pages = page_table.shape[1]
    assert n_pages % pages == 0 and pages % 2 == 0 and rows == SUBLANES
    steps = n_pages // pages
    pt = page_table.reshape(-1)
    nrow = MLA_HEADS * rows

    def page_spec(shape, i):
        return pl.BlockSpec((1, 1) + shape,
                            lambda bb, j, pt_r, _i=i: (e, pt_r[bb * n_pages + j * pages + _i], 0, 0))

    cache_kpe_t = jnp.transpose(cache_kpe, (0, 1, 3, 2))
    full = lambda a: pl.BlockSpec(a.shape, lambda bb, j, pt_r, _n=a.ndim: (0,) * _n)
    seq = lambda c: pl.BlockSpec((1, rows, c), lambda bb, j, pt_r: (bb, 0, 0))
    grid_spec = pltpu.PrefetchScalarGridSpec(
        num_scalar_prefetch=1, grid=(b, steps),
        in_specs=[seq(q.shape[-1]), seq(KV_LORA), seq(QK_ROPE), full(wukp), full(wuv)]
        + [page_spec((PAGE_SIZE, KV_LORA), i) for i in range(pages)]
        + [page_spec((QK_ROPE, PAGE_SIZE), i) for i in range(pages)],
        out_specs=seq(MLA_HEADS * V_HEAD),
        scratch_shapes=[pltpu.VMEM((nrow, KV_LORA), BF16), pltpu.VMEM((nrow, QK_ROPE), BF16),
                        pltpu.VMEM((nrow, LANES), F32), pltpu.VMEM((nrow, LANES), F32),
                        pltpu.VMEM((nrow, KV_LORA), F32)],
    )
    return pl.pallas_call(
        functools.partial(_mla_sample_kernel, pages=pages, rows=rows),
        out_shape=jax.ShapeDtypeStruct((b, rows, MLA_HEADS * V_HEAD), F32),
        grid_spec=grid_spec,
        compiler_params=_params("arbitrary", "arbitrary"),
        name="mla_sample",
    )(pt, q, ckv_new, kpe_new, wukp, wuv, *([cache_ckv] * pages), *([cache_kpe_t] * pages))


def _gelu_tanh(x):
    return x * (0.5 * (1.0 + jnp.tanh(0.7978845608028654 * (x + 0.044715 * (x * x * x)))))


def _lru_kernel(u_ref, gate_ref, cbuf_ref, h0_ref, cw_ref, cb_ref, wa_ref, ba_ref, wx_ref, bx_ref, lam_ref,
                out_ref, hlast_ref, cnew_ref, ext_scr, a_scr, b_scr, h_scr, hc_scr, *, gb, rb):
    j = pl.program_id(1)
    tm, w = gb * rb, LRU_WIDTH
    pre = SUBLANES
    tail = CONV_W - 1

    @pl.when(j == 0)
    def _first():
        ext_scr[:, 0:pre, :] = jnp.zeros((gb, pre, w), F32)
        ext_scr[:, pre - tail:pre, :] = cbuf_ref[...]
        hc_scr[...] = h0_ref[...]

    ext_scr[:, pre:pre + rb, :] = u_ref[...]
    uc = cb_ref[...][None]
    for t in range(CONV_W):
        uc = uc + ext_scr[:, pre - tail + t:pre - tail + t + rb, :] * cw_ref[t:t + 1, :][None]
    new_tail = ext_scr[:, pre + rb - tail:pre + rb, :]
    ext_scr[:, pre - tail:pre, :] = new_tail
    cnew_ref[...] = new_tail

    uc = uc.reshape(tm, w)
    ub = uc.astype(BF16)
    r = _sigmoid(_dot(ub, wa_ref[...]) + ba_ref[...])
    i = _sigmoid(_dot(ub, wx_ref[...]) + bx_ref[...])
    nl = -lam_ref[...]
    softplus = jnp.maximum(nl, 0.0) + jnp.log(1.0 + jnp.exp(-jnp.abs(nl)))
    log_a = (-LRU_C) * r * softplus
    a = jnp.exp(log_a)
    bt = jnp.sqrt(1.0 - jnp.exp(2.0 * log_a)) * (i * uc)

    rin = _imod(lax.broadcasted_iota(jnp.int32, (tm, w), 0), SUBLANES)
    for sft in (1, 2, 4):
        a_sh = pltpu.roll(a, sft, 0)
        b_sh = pltpu.roll(bt, sft, 0)
        m = rin >= sft
        bt = jnp.where(m, a * b_sh + bt, bt)
        a = jnp.where(m, a * a_sh, a)

    if rb == SUBLANES:
        hseq = a.reshape(gb, rb, w) * hc_scr[...] + bt.reshape(gb, rb, w)
        hc_scr[...] = hseq[:, rb - 1:rb, :]
    else:
        a_scr[...] = a
        b_scr[...] = bt

        def body(g, hprev):
            r0 = pl.multiple_of(g * SUBLANES, SUBLANES)
            hh = a_scr[pl.ds(r0, SUBLANES), :] * hprev + b_scr[pl.ds(r0, SUBLANES), :]
            h_scr[pl.ds(r0, SUBLANES), :] = hh
            return hh[SUBLANES - 1:SUBLANES, :]

        hc_scr[0] = lax.fori_loop(0, rb // SUBLANES, body, hc_scr[0])
        hseq = h_scr[...].reshape(gb, rb, w)
    hlast_ref[...] = hc_scr[...]
    out_ref[...] = (hseq * _gelu_tanh(gate_ref[...])).astype(out_ref.dtype)


def _lru(u, gate, conv_buf, h0, wts, tm_target):
    g, r, w = u.shape
    gb, rb = _token_tiles(g, r, tm_target)
    assert gb == 1 or rb == SUBLANES
    tm = gb * rb
    full = lambda a: pl.BlockSpec(a.shape, lambda i, j, _n=a.ndim: (0,) * _n)
    tok = pl.BlockSpec((gb, rb, w), lambda i, j: (i, j, 0))
    per_g = lambda rows: pl.BlockSpec((gb, rows, w), lambda i, j: (i, 0, 0))
    names = ["conv_w", "conv_b", "wa", "ba", "wx", "bx", "lam"]
    return pl.pallas_call(
        functools.partial(_lru_kernel, gb=gb, rb=rb),
        out_shape=[jax.ShapeDtypeStruct((g, r, w), _act_dtype(gb)), jax.ShapeDtypeStruct((g, 1, w), F32),
                   jax.ShapeDtypeStruct((g, CONV_W - 1, w), F32)],
        grid=(g // gb, r // rb),
        in_specs=[tok, tok, per_g(CONV_W - 1), per_g(1)] + [full(wts[n]) for n in names],
        out_specs=[tok, per_g(1), per_g(CONV_W - 1)],
        scratch_shapes=[pltpu.VMEM((gb, SUBLANES + rb, w), F32), pltpu.VMEM((tm, w), F32),
                        pltpu.VMEM((tm, w), F32), pltpu.VMEM((tm, w), F32), pltpu.VMEM((gb, 1, w), F32)],
        compiler_params=_params("arbitrary", "arbitrary"),
        name="lru",
    )(u, gate, conv_buf, h0, *[wts[n] for n in names])


def _mix_out_kernel(*refs, n_in, gb, rb, alpha):
    a_refs, w_refs = refs[:n_in], refs[n_in:2 * n_in]
    x_ref, gt_ref, lng_ref, lnb_ref, sc2_ref, sh2_ref, wr_ref, br_ref, x1_ref, h2_ref, lg_ref = refs[2 * n_in:]
    tm = gb * rb
    mix = None
    for a_ref, w_ref in zip(a_refs, w_refs):
        part = _dot(a_ref[...].reshape(tm, a_ref.shape[-1]).astype(BF16), w_ref[...])
        mix = part if mix is None else mix + part
    y = alpha * x_ref[...] + (1.0 + gt_ref[...]) * mix.reshape(gb, rb, D_MODEL)
    x1 = _layer_norm(y, lng_ref[...], lnb_ref[...])
    x1_ref[...] = x1
    h2 = x1 * (1.0 + sc2_ref[...]) + sh2_ref[...]
    h2_ref[...] = h2.astype(h2_ref.dtype)
    lg_ref[...] = _dot_nt(wr_ref[...], h2.reshape(tm, D_MODEL).astype(BF16)) + br_ref[...]


def _mix_out(acts, ws, x, gt, lng, lnb, sc2, sh2, wr, br, alpha, tm_target):
    g, r, d = x.shape
    gb, rb = _token_tiles(g, r, tm_target)
    tm = gb * rb
    nj = r // rb
    full = lambda a: pl.BlockSpec(a.shape, lambda i, j, _n=a.ndim: (0,) * _n)
    tok = lambda c: pl.BlockSpec((gb, rb, c), lambda i, j: (i, j, 0))
    mod = pl.BlockSpec((gb, 1, d), lambda i, j: (i, 0, 0))
    return pl.pallas_call(
        functools.partial(_mix_out_kernel, n_in=len(acts), gb=gb, rb=rb, alpha=alpha),
        out_shape=[jax.ShapeDtypeStruct((g, r, d), F32), jax.ShapeDtypeStruct((g, r, d), _act_dtype(gb)),
                   jax.ShapeDtypeStruct((ROUTER_ROWS, g * r), F32)],
        grid=(g // gb, nj),
        in_specs=[tok(a.shape[-1]) for a in acts] + [full(w) for w in ws]
        + [tok(d), mod, full(lng), full(lnb), mod, mod, full(wr), full(br)],
        out_specs=[tok(d), tok(d), pl.BlockSpec((ROUTER_ROWS, tm), lambda i, j: (0, i * nj + j))],
        compiler_params=_params("arbitrary", "arbitrary"),
        name="mix_out",
    )(*acts, *ws, x, gt, lng, lnb, sc2, sh2, wr, br)


def _route_kernel(lg_ref, comb_ref):
    tn = lg_ref.shape[1]
    big = 3.0e38
    grow = lax.broadcasted_iota(jnp.int32, (SUBLANES, tn), 0)
    gl = jnp.where(grow < N_GROUPS, lg_ref[0:SUBLANES, :], -big)
    gmax = jnp.max(gl, axis=0, keepdims=True)
    g_idx = jnp.min(jnp.where(gl == gmax, grow, N_GROUPS), axis=0, keepdims=True)
    g_w = 1.0 / jnp.sum(jnp.where(grow < N_GROUPS, jnp.exp(gl - gmax), 0.0), axis=0, keepdims=True)

    el = lg_ref[EXPERT_ROW0:EXPERT_ROW0 + N_EXPERTS, :]
    erow = lax.broadcasted_iota(jnp.int32, (N_EXPERTS, tn), 0)
    ingrp = _idiv(erow, EXP_PER_GROUP) == g_idx
    emax = jnp.max(jnp.where(ingrp, el, -big), axis=0, keepdims=True)
    ex = jnp.where(ingrp, jnp.exp(el - emax), 0.0)
    p = ex / jnp.sum(ex, axis=0, keepdims=True)
    ps = jnp.where(ingrp, p, -1.0)
    p1 = jnp.max(ps, axis=0, keepdims=True)
    i1 = jnp.min(jnp.where(ps == p1, erow, N_EXPERTS), axis=0, keepdims=True)
    ps2 = jnp.where(erow == i1, -1.0, ps)
    p2 = jnp.max(ps2, axis=0, keepdims=True)
    i2 = jnp.min(jnp.where(ps2 == p2, erow, N_EXPERTS), axis=0, keepdims=True)
    tot = p1 + p2
    comb_ref[...] = (jnp.where(erow == i1, g_w * p1 / tot, 0.0)
                     + jnp.where(erow == i2, g_w * p2 / tot, 0.0))


def _route(logits_t):
    n = logits_t.shape[1]
    tn = min(n, 2048)
    return pl.pallas_call(
        _route_kernel,
        out_shape=jax.ShapeDtypeStruct((N_EXPERTS, n), F32),
        grid=(n // tn,),
        in_specs=[pl.BlockSpec((ROUTER_ROWS, tn), lambda i: (0, i))],
        out_specs=pl.BlockSpec((N_EXPERTS, tn), lambda i: (0, i)),
        compiler_params=_params("arbitrary"),
        name="route",
    )(logits_t)


def _moe_kernel(h_ref, comb_ref, w1_ref, w3_ref, w2_ref, x1_ref, gt_ref, lng_ref, lnb_ref, out_ref, acc_scr,
                *, gb, rb, alpha):
    e = pl.program_id(2)
    tm = gb * rb

    @pl.when(e == 0)
    def _init():
        acc_scr[...] = jnp.zeros(acc_scr.shape, F32)

    h = h_ref[...].reshape(tm, D_MODEL).astype(BF16)
    comb = comb_ref[...].reshape(tm, N_EXPERTS)
    lane = lax.broadcasted_iota(jnp.int32, (tm, N_EXPERTS), 1)
    nx = w1_ref.shape[0]
    ces = [jnp.sum(jnp.where(lane == e * nx + x, comb, 0.0), axis=1, keepdims=True) for x in range(nx)]
    rc = min(tm, MOE_ROW_CHUNK)
    for r0 in range(0, tm, rc):
        c = slice(r0, r0 + rc)
        ab = [(_dot(h[c], w1_ref[x]), _dot(h[c], w3_ref[x])) for x in range(nx)]
        hid = [((a * _sigmoid(a)) * b * ces[x][c]).astype(BF16) for x, (a, b) in enumerate(ab)]
        y = None
        for x in range(nx):
            d = _dot(hid[x], w2_ref[x])
            y = d if y is None else y + d
        acc_scr[c, :] += y

    @pl.when(e == pl.num_programs(2) - 1)
    def _fin():
        y = alpha * x1_ref[...] + (1.0 + gt_ref[...]) * acc_scr[...].reshape(gb, rb, D_MODEL)
        out_ref[...] = _layer_norm(y, lng_ref[...], lnb_ref[...])


def _moe(h2, comb, layer, w1, w3, w2, x1, gt, lng, lnb, alpha, tm_target):
    g, r, d = x1.shape
    gb, rb = _token_tiles(g, r, tm_target)
    nx = MOE_EXPERTS_PER_STEP
    assert w1.shape[1] % nx == 0
    tok = lambda c: pl.BlockSpec((gb, rb, c), lambda i, j, e: (i, j, 0))
    full = lambda a: pl.BlockSpec(a.shape, lambda i, j, e, _n=a.ndim: (0,) * _n)
    wspec = lambda a: pl.BlockSpec((None, nx) + a.shape[2:], lambda i, j, e: (layer, e, 0, 0))
    return pl.pallas_call(
        functools.partial(_moe_kernel, gb=gb, rb=rb, alpha=alpha),
        out_shape=jax.ShapeDtypeStruct((g, r, d), F32),
        grid=(g // gb, r // rb, w1.shape[1] // nx),
        in_specs=[tok(d), tok(N_EXPERTS), wspec(w1), wspec(w3), wspec(w2), tok(d),
                  pl.BlockSpec((gb, 1, d), lambda i, j, e: (i, 0, 0)), full(lng), full(lnb)],
        out_specs=tok(d),
        scratch_shapes=[pltpu.VMEM((gb * rb, d), F32)],
        compiler_params=_params("arbitrary", "arbitrary", "arbitrary"),
        name="moe",
    )(h2, comb, w1, w3, w2, x1, gt, lng, lnb)


def _sb_qkv_kernel(x_ref, sc_ref, sh_ref, w_ref, *outs, prompt, gb, rb):
    tm = gb * rb
    h = (x_ref[...] * (1.0 + sc_ref[...]) + sh_ref[...]).reshape(tm, D_MODEL).astype(BF16)
    qkv = _dot(h, w_ref[...])
    q = qkv[:, :D_MODEL] * SB_SCALE
    k = qkv[:, D_MODEL:2 * D_MODEL]
    v = qkv[:, 2 * D_MODEL:]
    if prompt:
        k_ref, v_ref, qh_ref, kh_ref, vh_ref = outs
        for hp in range(SB_HEADS // 2):
            sl = slice(hp * LANES, (hp + 1) * LANES)
            qh_ref[0, hp] = q[:, sl].astype(BF16)
            kh_ref[0, hp] = k[:, sl].astype(BF16)
            vh_ref[0, hp] = v[:, sl].astype(BF16)
    else:
        k_ref, v_ref, q_ref = outs
        q_ref[...] = q.reshape(gb, rb, D_MODEL)
    k_ref[...] = k.reshape(gb, rb, D_MODEL)
    v_ref[...] = v.reshape(gb, rb, D_MODEL)


def _sb_qkv(x, sc, sh, w, prompt, tm_target):
    g, r, d = x.shape
    gb, rb = _token_tiles(g, r, tm_target)
    tok = pl.BlockSpec((gb, rb, d), lambda i, j: (i, j, 0))
    mod = pl.BlockSpec((gb, 1, d), lambda i, j: (i, 0, 0))
    out_shape = [jax.ShapeDtypeStruct((g, r, d), F32)] * 2
    out_specs = [tok, tok]
    if prompt:
        assert gb == 1
        npair = SB_HEADS // 2
        out_shape += [jax.ShapeDtypeStruct((g, npair, r, LANES), BF16)] * 3
        out_specs += [pl.BlockSpec((1, npair, rb, LANES), lambda i, j: (i, 0, j, 0))] * 3
    else:
        out_shape += [jax.ShapeDtypeStruct((g, r, d), F32)]
        out_specs += [tok]
    return pl.pallas_call(
        functools.partial(_sb_qkv_kernel, prompt=prompt, gb=gb, rb=rb),
        out_shape=out_shape, grid=(g // gb, r // rb),
        in_specs=[tok, mod, mod, pl.BlockSpec(w.shape, lambda i, j: (0, 0))],
        out_specs=out_specs,
        compiler_params=_params("arbitrary", "arbitrary"),
        name="sb_qkv_prompt" if prompt else "sb_qkv_sample",
    )(x, sc, sh, w)


def _sb_terms(z, upper, valid):
    lq = jnp.minimum(z, 0.0) - jnp.log(1.0 + jnp.exp(-jnp.abs(z)))
    lk = lq - z
    if valid is not None:
        lk = jnp.where(valid, lk, 0.0)
    hi = lk.astype(BF16)
    lo = (lk - hi.astype(F32)).astype(BF16)
    return lq + (_dot(hi, upper) + _dot(lo, upper)), jnp.sum(lk, axis=-1, keepdims=True)


def _sb_weights(arg, carry, valid):
    w = jnp.exp(arg + _lanes(carry, arg.shape[1]))
    if valid is not None:
        w = jnp.where(valid, w, 0.0)
    return w.astype(BF16)


def _sb_prompt_kernel(qi_ref, kj_ref, q_ref, k_ref, v_ref, up_ref, o_ref, carry_scr, acc_scr, *, tq):
    t = pl.program_id(1)
    qi, kj = qi_ref[t], kj_ref[t]
    npair = SB_HEADS // 2

    @pl.when(kj == qi)
    def _init():
        carry_scr[...] = jnp.zeros(carry_scr.shape, F32)
        acc_scr[...] = jnp.zeros(acc_scr.shape, F32)

    kb = up_ref.shape[0]

    def run(diagonal):
        valid = None
        if diagonal:
            row = lax.broadcasted_iota(jnp.int32, (tq, tq), 0)
            col = lax.broadcasted_iota(jnp.int32, (tq, tq), 1)
            valid = col < row
        lane = lax.broadcasted_iota(jnp.int32, (tq, LANES), 1)
        upper = up_ref[...]

        hsels = [lane < SB_HEAD_DIM, lane >= SB_HEAD_DIM]
        k0s = list(range(tq - kb, -1, -kb))
        r0s = [k0 if diagonal else 0 for k0 in k0s]
        vlds = [None if valid is None else valid[r0:, k0:k0 + kb] for k0, r0 in zip(k0s, r0s)]

        def pad_rows(x, r0):
            return x if r0 == 0 else jnp.concatenate([jnp.zeros((r0, x.shape[1]), x.dtype), x], axis=0)

        def body(i, c):
            hps = [i * SB_PAIRS_PER_TRIP + n for n in range(SB_PAIRS_PER_TRIP)]
            zs = []
            for hp in hps:
                qq, kk = q_ref[0, hp].astype(F32), k_ref[0, hp]
                zs.append([_dot_nt(jnp.where(hsel, qq, 0.0).astype(BF16), kk) for hsel in hsels])
            terms = [[[_sb_terms(z[r0:, k0:k0 + kb], upper, vld) for k0, r0, vld in zip(k0s, r0s, vlds)]
                      for z in zp] for zp in zs]
            for hp, tp in zip(hps, terms):
                vv = v_ref[0, hp]
                out = None
                for sub in range(2):
                    carry = carry_scr[hp, sub]
                    pv = None
                    for (arg, total), k0, r0, vld in zip(tp[sub], k0s, r0s, vlds):
                        d = pad_rows(_dot(_sb_weights(arg, carry[r0:], vld), vv[k0:k0 + kb]), r0)
                        pv = d if pv is None else pv + d
                        carry = carry + pad_rows(total, r0)
                    carry_scr[hp, sub] = carry
                    out = jnp.where(hsels[sub], pv, 0.0 if out is None else out)
                acc_scr[hp] += out
            return c

        lax.fori_loop(0, npair // SB_PAIRS_PER_TRIP, body, 0)

    @pl.when(kj < qi)
    def _off():
        run(False)

    @pl.when(kj == qi)
    def _diag():
        run(True)

    @pl.when(kj == 0)
    def _fin():
        for hp in range(npair):
            o_ref[0, :, hp * LANES:(hp + 1) * LANES] = acc_scr[hp].astype(BF16)


def _upper(n):
    j = np.arange(n)[:, None]
    s = np.arange(n)[None, :]
    return jnp.asarray((j > s).astype(np.float32), dtype=BF16)


def _sb_prompt(q, k, v, tq):
    b, npair, s, _ = q.shape
    nq = s // tq
    qi, kj = _triangle(nq, descending=True)
    kb = SB_CUMSUM_KEYS if tq % SB_CUMSUM_KEYS == 0 else tq
    up = _upper(kb)
    blk = lambda sel: pl.BlockSpec((1, npair, tq, LANES), sel)
    grid_spec = pltpu.PrefetchScalarGridSpec(
        num_scalar_prefetch=2, grid=(b, qi.shape[0]),
        in_specs=[blk(lambda bb, t, qi_r, kj_r: (bb, 0, qi_r[t], 0)),
                  blk(lambda bb, t, qi_r, kj_r: (bb, 0, kj_r[t], 0)),
                  blk(lambda bb, t, qi_r, kj_r: (bb, 0, kj_r[t], 0)),
                  pl.BlockSpec((kb, kb), lambda bb, t, qi_r, kj_r: (0, 0))],
        out_specs=pl.BlockSpec((1, tq, npair * LANES), lambda bb, t, qi_r, kj_r: (bb, qi_r[t], 0)),
        scratch_shapes=[pltpu.VMEM((npair, 2, tq, LANES), F32), pltpu.VMEM((npair, tq, LANES), F32)],
    )
    return pl.pallas_call(
        functools.partial(_sb_prompt_kernel, tq=tq),
        out_shape=jax.ShapeDtypeStruct((b, s, npair * LANES), BF16),
        grid_spec=grid_spec,
        compiler_params=_params("arbitrary", "arbitrary"),
        name="sb_prompt",
    )(qi, kj, q, k, v, up)


def _sb_sample_kernel(pt_ref, q_ref, kn_ref, vn_ref, up_ref, *rest, pages, rows):
    k_refs, v_refs = rest[:pages], rest[pages:2 * pages]
    o_ref, qbd_scr, carry_scr, acc_scr = rest[2 * pages:]
    j = pl.program_id(1)
    nrow = SB_HEADS * rows
    upper = up_ref[...]

    @pl.when(j == 0)
    def _first():
        qt = jnp.concatenate([q_ref[0]] * SB_HEADS, axis=0)
        rgrp = _idiv(lax.broadcasted_iota(jnp.int32, (nrow, D_MODEL), 0), rows)
        lgrp = _idiv(lax.broadcasted_iota(jnp.int32, (nrow, D_MODEL), 1), SB_HEAD_DIM)
        qbd = jnp.where(rgrp == lgrp, qt, 0.0).astype(BF16)
        qbd_scr[...] = qbd
        pad = jnp.zeros((PAGE_SIZE - rows, D_MODEL), F32)
        kn = jnp.concatenate([kn_ref[0], pad], axis=0).astype(BF16)
        vn = jnp.concatenate([vn_ref[0], pad], axis=0).astype(BF16)
        row = lax.broadcasted_iota(jnp.int32, (nrow, PAGE_SIZE), 0)
        col = lax.broadcasted_iota(jnp.int32, (nrow, PAGE_SIZE), 1)
        valid = col < _imod(row, rows)
        arg, total = _sb_terms(_dot_nt(qbd, kn), upper[:PAGE_SIZE, :PAGE_SIZE], valid)
        w = _sb_weights(arg, jnp.zeros((nrow, LANES), F32), valid)
        carry_scr[...] = jnp.broadcast_to(total, (nrow, LANES))
        acc_scr[...] = _dot(w, vn)

    qbd = qbd_scr[...]
    pairs = range(0, pages, 2)
    zs = [_dot(qbd, jnp.concatenate([k_refs[i + 1][0, 0], k_refs[i][0, 0]], axis=1).astype(BF16))
          for i in pairs]
    terms = [_sb_terms(z, upper, None) for z in zs]
    carry = carry_scr[...]
    pv = None
    for i, (arg, total) in zip(pairs, terms):
        vt = jnp.concatenate([v_refs[i + 1][0, 0], v_refs[i][0, 0]], axis=1).astype(BF16)
        d = _dot_nt(_sb_weights(arg, carry, None), vt)
        pv = d if pv is None else pv + d
        carry = carry + total
    carry_scr[...] = carry
    acc_scr[...] += pv

    @pl.when(j == pl.num_programs(1) - 1)
    def _last():
        lgrp = _idiv(lax.broadcasted_iota(jnp.int32, (rows, D_MODEL), 1), SB_HEAD_DIM)
        out = jnp.zeros((rows, D_MODEL), F32)
        for hd in range(SB_HEADS):
            out = out + jnp.where(lgrp == hd, acc_scr[hd * rows:(hd + 1) * rows, :], 0.0)
        o_ref[0] = out


def _sb_sample(q, k_new, v_new, cache_k, cache_v, o_idx, page_table, pages):
    b, rows, d = q.shape
    n_pages = page_table.shape[1]
    assert n_pages % pages == 0 and pages % 2 == 0 and rows == SUBLANES
    steps = n_pages // pages
    pt = page_table.reshape(-1)
    nrow = SB_HEADS * rows
    up = _upper(2 * PAGE_SIZE)

    def page_spec(i):
        return pl.BlockSpec(
            (1, 1, d, PAGE_SIZE),
            lambda bb, j, pt_r, _i=i: (o_idx, pt_r[bb * n_pages + n_pages - 1 - (j * pages + _i)], 0, 0))

    seq = pl.BlockSpec((1, rows, d), lambda bb, j, pt_r: (bb, 0, 0))
    grid_spec = pltpu.PrefetchScalarGridSpec(
        num_scalar_prefetch=1, grid=(b, steps),
        in_specs=[seq, seq, seq, pl.BlockSpec(up.shape, lambda bb, j, pt_r: (0, 0))]
        + [page_spec(i) for i in range(pages)] * 2,
        out_specs=seq,
        scratch_shapes=[pltpu.VMEM((nrow, d), BF16), pltpu.VMEM((nrow, LANES), F32), pltpu.VMEM((nrow, d), F32)],
    )
    return pl.pallas_call(
        functools.partial(_sb_sample_kernel, pages=pages, rows=rows),
        out_shape=jax.ShapeDtypeStruct((b, rows, d), F32),
        grid_spec=grid_spec,
        compiler_params=_params("arbitrary", "arbitrary"),
        name="sb_sample",
    )(pt, q, k_new, v_new, up, *([cache_k] * pages), *([cache_v] * pages))


def _prep_layer0(e, w_in_ab, g_q, w_uq, g_kv, w_uk, w_uv, conv_w, conv_b, w_rg_a, b_rg_a, w_rg_x, b_rg_x,
                 lru_lambda):
    half = QK_ROPE // 2
    w_in = w_in_ab[e]
    o1, o2, o3, o4 = Q_LORA, Q_LORA + KV_LORA, Q_LORA + KV_LORA + QK_ROPE, Q_LORA + KV_LORA + QK_ROPE + LRU_WIDTH
    kpe_w = w_in[:, o2:o3]
    zpad = jnp.zeros((D_MODEL, LANES - QK_ROPE), F32)
    win = jnp.concatenate([
        w_in[:, :o2], w_in[:, o3:o4], w_in[:, o4:],
        kpe_w, zpad,
        kpe_w[:, half:], kpe_w[:, :half], zpad], axis=1).astype(BF16)

    wq = w_uq[e].reshape(Q_LORA, MLA_HEADS, QK_NOPE + QK_ROPE)
    nope, x1, x2 = wq[..., :QK_NOPE], wq[..., QK_NOPE:QK_NOPE + half], wq[..., QK_NOPE + half:]
    z32 = jnp.zeros((Q_LORA, MLA_HEADS, HEAD_LANES - QK_NOPE - QK_ROPE), F32)
    wqa = jnp.concatenate([x1, x2, z32, nope], axis=-1).reshape(Q_LORA, MLA_HEADS * HEAD_LANES).astype(BF16)
    wqb = jnp.concatenate([x2, x1, z32, jnp.zeros_like(nope)], axis=-1)
    wqb = wqb.reshape(Q_LORA, MLA_HEADS * HEAD_LANES).astype(BF16)

    wk_top = jnp.concatenate([jnp.zeros((KV_LORA, MLA_HEADS, HEAD_LANES - QK_NOPE), F32), w_uk[e]], axis=-1)
    eye = jnp.eye(LANES, HEAD_LANES, dtype=F32) * (jnp.arange(LANES) < QK_ROPE)[:, None]
    wk_bot = jnp.broadcast_to(eye[:, None, :], (LANES, MLA_HEADS, HEAD_LANES))
    wkp = jnp.concatenate([wk_top, wk_bot], axis=0).reshape(KV_LORA + LANES, MLA_HEADS * HEAD_LANES).astype(BF16)
    wv = w_uv[e].reshape(KV_LORA, MLA_HEADS * V_HEAD).astype(BF16)
    wukp = jnp.concatenate([jnp.zeros((MLA_HEADS, HEAD_LANES - QK_NOPE, KV_LORA), F32),
                            jnp.transpose(w_uk[e], (1, 2, 0))], axis=1).astype(BF16)

    def block_diag(wb):
        n, k, _ = wb.shape
        eye_n = jnp.eye(n, dtype=F32)
        return (wb[:, :, None, :] * eye_n[:, None, :, None]).reshape(n * k, n * k).astype(BF16)

    row = lambda v: v.reshape(1, -1)
    return dict(
        win=win, gq=row(g_q[e]), gkv=row(g_kv[e]), wqa=wqa, wqb=wqb, wkp=wkp, wv=wv, wukp=wukp,
        conv_w=conv_w[e], conv_b=row(conv_b[e]), wa=block_diag(w_rg_a[e]), ba=row(b_rg_a[e]),
        wx=block_diag(w_rg_x[e]), bx=row(b_rg_x[e]), lam=row(lru_lambda[e]))


def _rope_tables(pos):
    half = QK_ROPE // 2
    inv = ROPE_THETA ** (-jnp.arange(half, dtype=F32) / half)
    ang = pos.astype(F32)[:, None] * inv
    cos, sin = jnp.cos(ang), jnp.sin(ang)
    n = pos.shape[0]
    z32 = jnp.zeros((n, HEAD_LANES - QK_NOPE - QK_ROPE), F32)
    ctab = jnp.concatenate([cos, cos, z32, jnp.ones((n, QK_NOPE), F32)], axis=1)
    stab = jnp.concatenate([-sin, sin, z32, jnp.zeros((n, QK_NOPE), F32)], axis=1)
    return ctab, stab


def _router_weights(w_rg, b_rg, w_re, b_re):
    wr = jnp.zeros((ROUTER_ROWS, D_MODEL), F32)
    wr = wr.at[:N_GROUPS].set(w_rg.T).at[EXPERT_ROW0:EXPERT_ROW0 + N_EXPERTS].set(w_re.T)
    br = jnp.zeros((ROUTER_ROWS, 1), F32)
    br = br.at[:N_GROUPS, 0].set(b_rg).at[EXPERT_ROW0:EXPERT_ROW0 + N_EXPERTS, 0].set(b_re)
    return wr.astype(BF16), br


def _run_group(x, c, pos, past, prep, tiles):
    depth = len(prep["layers"])
    alpha = (2.0 * depth) ** 0.25
    prompt = past is None
    g, r, d = x.shape
    mods = _adaln(c, prep["w_mod"], prep["b_mod"])
    ctab, stab = _rope_tables(pos)
    ckv_l, kpe_l, h_l, conv_l, k_l, v_l = [], [], [], [], [], []
    for l in range(depth):
        sh1, sc1, gt1, sh2, sc2, gt2 = mods[l]
        lw = prep["layers"][l]
        if l % 2 == 0:
            e = l // 2
            w0 = lw["mixer"]
            if prompt:
                q, kf, vv, ckv, kpe, u, gate = _ab_in(x, sc1, sh1, w0, ctab, stab, True, tiles["tok"])
                attn = _mla_prompt(q, kf, vv, tiles["attn"])
                h0 = jnp.zeros((g, 1, LRU_WIDTH), F32)
                conv_buf = jnp.zeros((g, CONV_W - 1, LRU_WIDTH), F32)
            else:
                cache_ckv, cache_kpe, st_h, st_conv, _, _, page_table = past
                q, ckv, kpe, u, gate = _ab_in(x, sc1, sh1, w0, ctab, stab, False, tiles["tok"])
                attn = _mla_sample(q, ckv, kpe, cache_ckv, cache_kpe, e, page_table, w0["wukp"], w0["wv"],
                                   tiles["mla_pages"])
                h0 = st_h[e][:, None, :]
                conv_buf = st_conv[e]
            lru_out, h_last, conv_new = _lru(u, gate, conv_buf, h0, w0, tiles["lru"])
            acts, ws = [attn, lru_out], [lw["w_out_a"], lw["w_out_b"]]
            ckv_l.append(ckv)
            kpe_l.append(kpe)
            h_l.append(h_last[:, 0, :])
            conv_l.append(conv_new)
        else:
            o = l // 2
            if prompt:
                k, v, qh, kh, vh = _sb_qkv(x, sc1, sh1, lw["w_qkv"], True, tiles["tok"])
                att = _sb_prompt(qh, kh, vh, tiles["attn"])
            else:
                k, v, q = _sb_qkv(x, sc1, sh1, lw["w_qkv"], False, tiles["tok"])
                att = _sb_sample(q, k, v, past[4], past[5], o, past[6], tiles["sb_pages"])
            acts, ws = [att], [lw["w_out"]]
            k_l.append(k.reshape(g, r, SB_HEADS, SB_HEAD_DIM))
            v_l.append(v.reshape(g, r, SB_HEADS, SB_HEAD_DIM))
        x1, h2, logits_t = _mix_out(acts, ws, x, gt1, lw["ln_g1"], lw["ln_b1"], sc2, sh2, lw["wr"], lw["br"],
                                    alpha, tiles["tok"])
        comb = _route(logits_t).T.reshape(g, r, N_EXPERTS)
        x = _moe(h2, comb, l, prep["w1"], prep["w3"], prep["w2"], x1, gt2, lw["ln_g2"], lw["ln_b2"], alpha,
                 tiles["moe"])
    return x, (jnp.stack(ckv_l), jnp.stack(kpe_l), jnp.stack(h_l), jnp.stack(conv_l), jnp.stack(k_l),
               jnp.stack(v_l))


def kernel(x_prompt, x_sample, cache_mla_ckv, cache_mla_kpe, cache_sb_k, cache_sb_v, state_lru_h, state_conv,
           page_table, c_prompt, c_sample, w_mod, b_mod, ln_g, ln_b, w_in_ab, g_q, w_uq, g_kv, w_uk, w_uv,
           conv_w, conv_b, w_rg_a, b_rg_a, w_rg_x, b_rg_x, lru_lambda, w_out_ab, w_qkv_c, w_out_c,
           w_router_g, b_router_g, w_router_e, b_router_e, w_e1, w_e3, w_e2):
    depth = w_mod.shape[0]
    row = lambda v: v.reshape(1, -1)
    layers = []
    for l in range(depth):
        wr, br = _router_weights(w_router_g[l], b_router_g[l], w_router_e[l], b_router_e[l])
        lw = dict(ln_g1=row(ln_g[l, 0]), ln_b1=row(ln_b[l, 0]), ln_g2=row(ln_g[l, 1]), ln_b2=row(ln_b[l, 1]),
                  wr=wr, br=br)
        if l % 2 == 0:
            e = l // 2
            lw["mixer"] = _prep_layer0(e, w_in_ab, g_q, w_uq, g_kv, w_uk, w_uv, conv_w, conv_b, w_rg_a, b_rg_a,
                                       w_rg_x, b_rg_x, lru_lambda)
            wo = w_out_ab[e].astype(BF16)
            lw["w_out_a"], lw["w_out_b"] = wo[:MLA_HEADS * V_HEAD], wo[MLA_HEADS * V_HEAD:]
        else:
            o = l // 2
            lw["w_qkv"] = w_qkv_c[o].astype(BF16)
            lw["w_out"] = w_out_c[o].astype(BF16)
        layers.append(lw)
    prep = dict(w_mod=w_mod, b_mod=b_mod, layers=layers,
                w1=w_e1.astype(BF16), w3=w_e3.astype(BF16), w2=w_e2.astype(BF16))

    n_pool = cache_sb_k.shape[1]
    cache_k = jnp.transpose(cache_sb_k, (0, 1, 3, 4, 2)).reshape(cache_sb_k.shape[0], n_pool, D_MODEL, PAGE_SIZE)
    cache_v = jnp.transpose(cache_sb_v, (0, 1, 3, 4, 2)).reshape(cache_sb_v.shape[0], n_pool, D_MODEL, PAGE_SIZE)
    n_pages = page_table.shape[1]
    past = (cache_mla_ckv, cache_mla_kpe, state_lru_h, state_conv, cache_k, cache_v, page_table)

    seq = x_prompt.shape[1]
    tiles_p = dict(tok=min(256, seq), attn=min(512, seq), lru=min(512, seq), moe=min(1024, seq))
    n_s = x_sample.shape[0] * x_sample.shape[1]
    tiles_s = dict(tok=min(256, n_s), lru=n_s, moe=min(1024, n_s),
                   mla_pages=min(64, n_pages), sb_pages=min(8, n_pages))

    pos_p = jnp.arange(seq, dtype=jnp.int32)
    pos_s = n_pages * PAGE_SIZE + jnp.arange(x_sample.shape[1], dtype=jnp.int32)
    y_p, (ckv_p, kpe_p, h_p, conv_p, k_p, v_p) = _run_group(x_prompt, c_prompt, pos_p, None, prep, tiles_p)
    y_s, (ckv_s, kpe_s, h_s, conv_s, k_s, v_s) = _run_group(x_sample, c_sample, pos_s, past, prep, tiles_s)
    return (y_p, y_s, ckv_p, kpe_p, h_p, conv_p, k_p, v_p, ckv_s, kpe_s, h_s, conv_s, k_s, v_s)
```

```python
import functools

import numpy as np
import jax
import jax.numpy as jnp
from jax import lax
from jax.experimental import pallas as pl
from jax.experimental.pallas import tpu as pltpu

F32, BF16 = jnp.float32, jnp.bfloat16

D_MODEL = 1024
MLA_HEADS, QK_NOPE, QK_ROPE, V_HEAD = 8, 64, 32, 64
Q_LORA, KV_LORA = 384, 256
ROPE_THETA = 10000.0
MLA_SCALE = (QK_NOPE + QK_ROPE) ** -0.5
LRU_WIDTH, LRU_BLOCKS, CONV_W, LRU_C = 512, 8, 4, 8.0
SB_HEADS, SB_HEAD_DIM = 16, 64
SB_SCALE = SB_HEAD_DIM ** -0.5
N_GROUPS, EXP_PER_GROUP, N_EXPERTS, D_EXPERT = 4, 4, 16, 512
LN_EPS, RMS_EPS = 1e-5, 1e-6
NEG_INF = -1e30
PAGE_SIZE = 128

LANES = 128
SUBLANES = 8
VMEM_LIMIT_BYTES = 56 * 1024 * 1024

MXU_DIM = 256
SB_CUMSUM_KEYS = MXU_DIM
MOE_ROW_CHUNK = MXU_DIM
MOE_EXPERTS_PER_STEP = 4
SB_PAIRS_PER_TRIP = 4
MLA_PAGES_PER_GROUP = 16

HEAD_LANES = LANES
ROUTER_ROWS = 128
EXPERT_ROW0 = 8


def _dot(a, b):
    return jnp.dot(a, b, preferred_element_type=F32)


def _dot_nt(a, b):
    return lax.dot_general(a, b, (((1,), (1,)), ((), ())), preferred_element_type=F32)


def _sigmoid(x):
    return 1.0 / (1.0 + jnp.exp(-x))


def _layer_norm(y, g, b):
    mu = jnp.mean(y, axis=-1, keepdims=True)
    d = y - mu
    var = jnp.mean(d * d, axis=-1, keepdims=True)
    return d * lax.rsqrt(var + LN_EPS) * g + b


def _rms_norm(x, g):
    return x * lax.rsqrt(jnp.mean(x * x, axis=-1, keepdims=True) + RMS_EPS) * g


def _shift_of(n):
    s = int(n).bit_length() - 1
    assert 1 << s == n
    return s


def _idiv(x, n):
    return lax.shift_right_logical(x, _shift_of(n))


def _imod(x, n):
    assert 1 << _shift_of(n) == n
    return x & (n - 1)


def _act_dtype(gb):
    return BF16 if gb == 1 else F32


def _lanes(x, width):
    n = width // LANES
    return x if n == 1 else jnp.concatenate([x] * n, axis=1)


def _params(*sem):
    return pltpu.CompilerParams(dimension_semantics=sem, vmem_limit_bytes=VMEM_LIMIT_BYTES)


def _token_tiles(groups, rows, target):
    if rows >= target:
        assert rows % target == 0
        return 1, target
    gb = min(groups, max(1, target // rows))
    assert groups % gb == 0
    return gb, rows


def _adaln_kernel(c_ref, w_ref, b_ref, o_ref):
    c = c_ref[...]
    s = (c * _sigmoid(c)).astype(BF16)
    o_ref[0] = _dot(s, w_ref[0].astype(BF16)) + b_ref[0]


def _adaln(c, w_mod, b_mod):
    depth, d, d6 = w_mod.shape
    b = c.shape[0]
    bp = -(-b // SUBLANES) * SUBLANES
    cp = jnp.pad(c, ((0, bp - b), (0, 0)))
    tn = 1536
    out = pl.pallas_call(
        _adaln_kernel,
        out_shape=jax.ShapeDtypeStruct((depth, bp, d6), F32),
        grid=(depth, d6 // tn),
        in_specs=[
            pl.BlockSpec((bp, d), lambda l, n: (0, 0)),
            pl.BlockSpec((1, d, tn), lambda l, n: (l, 0, n)),
            pl.BlockSpec((1, 1, tn), lambda l, n: (l, 0, n)),
        ],
        out_specs=pl.BlockSpec((1, bp, tn), lambda l, n: (l, 0, n)),
        compiler_params=_params("arbitrary", "arbitrary"),
        name="adaln",
    )(cp, w_mod, b_mod.reshape(depth, 1, d6))
    m = out[:, :b].reshape(depth, b, 6, 1, d)
    return [[m[l, :, i] for i in range(6)] for l in range(depth)]


def _ab_in_kernel(x_ref, sc_ref, sh_ref, win_ref, gq_ref, gkv_ref, wqa_ref, wqb_ref, ct_ref, st_ref,
                  *rest, prompt, gb, rb):
    if prompt:
        wkp_ref, wv_ref, q_ref, kf_ref, v_ref, ckv_ref, kpe_ref, u_ref, gate_ref = rest
    else:
        q_ref, ckv_ref, kpe_ref, u_ref, gate_ref = rest
    tm = gb * rb
    h = (x_ref[...] * (1.0 + sc_ref[...]) + sh_ref[...]).reshape(tm, D_MODEL).astype(BF16)
    p = _dot(h, win_ref[...])
    o_kv, o_u, o_g, o_ka, o_kb = Q_LORA, Q_LORA + KV_LORA, Q_LORA + KV_LORA + LRU_WIDTH, \
        Q_LORA + KV_LORA + 2 * LRU_WIDTH, Q_LORA + KV_LORA + 2 * LRU_WIDTH + LANES
    q_lat, kv_lat = p[:, :o_kv], p[:, o_kv:o_u]
    u_ref[...] = p[:, o_u:o_g].reshape(gb, rb, LRU_WIDTH)
    gate_ref[...] = p[:, o_g:o_ka].reshape(gb, rb, LRU_WIDTH)
    kpe_a, kpe_b = p[:, o_ka:o_kb], p[:, o_kb:o_kb + LANES]

    c = jnp.broadcast_to(ct_ref[...][None], (gb, rb, LANES)).reshape(tm, LANES)
    s = jnp.broadcast_to(st_ref[...][None], (gb, rb, LANES)).reshape(tm, LANES)

    qn = _rms_norm(q_lat, gq_ref[...]).astype(BF16)
    qa = _dot(qn, wqa_ref[...])
    qb = _dot(qn, wqb_ref[...])
    ckv = _rms_norm(kv_lat, gkv_ref[...])
    ckv_ref[...] = ckv.reshape(gb, rb, KV_LORA)
    kpe = kpe_a * c + kpe_b * s
    kpe_ref[...] = kpe[:, :QK_ROPE].reshape(gb, rb, QK_ROPE)

    for hd in range(MLA_HEADS):
        sl = slice(hd * HEAD_LANES, (hd + 1) * HEAD_LANES)
        qh = qa[:, sl] * c + qb[:, sl] * s
        if prompt:
            q_ref[0, hd] = qh.astype(BF16)
        else:
            q_ref[:, :, sl] = qh.reshape(gb, rb, HEAD_LANES)
    if prompt:
        ckv_bf = ckv.astype(BF16)
        kin = jnp.concatenate([ckv_bf, kpe.astype(BF16)], axis=-1)
        kf = _dot(kin, wkp_ref[...])
        vv = _dot(ckv_bf, wv_ref[...])
        for hd in range(MLA_HEADS):
            kf_ref[0, hd] = kf[:, hd * HEAD_LANES:(hd + 1) * HEAD_LANES].astype(BF16)
        for hp in range(MLA_HEADS // 2):
            v_ref[0, hp] = vv[:, hp * LANES:(hp + 1) * LANES].astype(BF16)


def _ab_in(x, sc, sh, wts, ctab, stab, prompt, tm_target):
    g, r, d = x.shape
    gb, rb = _token_tiles(g, r, tm_target)
    grid = (g // gb, r // rb)
    full = lambda a: pl.BlockSpec(a.shape, lambda i, j, _n=a.ndim: (0,) * _n)
    tok = lambda c: pl.BlockSpec((gb, rb, c), lambda i, j: (i, j, 0))
    mod = pl.BlockSpec((gb, 1, d), lambda i, j: (i, 0, 0))
    tab = pl.BlockSpec((rb, LANES), lambda i, j: (j, 0))
    ins = [x, sc, sh, wts["win"], wts["gq"], wts["gkv"], wts["wqa"], wts["wqb"], ctab, stab]
    in_specs = [tok(d), mod, mod, full(wts["win"]), full(wts["gq"]), full(wts["gkv"]),
                full(wts["wqa"]), full(wts["wqb"]), tab, tab]
    tail_shapes = [jax.ShapeDtypeStruct((g, r, KV_LORA), F32), jax.ShapeDtypeStruct((g, r, QK_ROPE), F32),
                   jax.ShapeDtypeStruct((g, r, LRU_WIDTH), F32), jax.ShapeDtypeStruct((g, r, LRU_WIDTH), F32)]
    tail_specs = [tok(KV_LORA), tok(QK_ROPE), tok(LRU_WIDTH), tok(LRU_WIDTH)]
    if prompt:
        assert gb == 1
        ins += [wts["wkp"], wts["wv"]]
        in_specs += [full(wts["wkp"]), full(wts["wv"])]
        hm = lambda nh: pl.BlockSpec((1, nh, rb, LANES), lambda i, j: (i, 0, j, 0))
        out_shape = [jax.ShapeDtypeStruct((g, MLA_HEADS, r, LANES), BF16),
                     jax.ShapeDtypeStruct((g, MLA_HEADS, r, LANES), BF16),
                     jax.ShapeDtypeStruct((g, MLA_HEADS // 2, r, LANES), BF16)] + tail_shapes
        out_specs = [hm(MLA_HEADS), hm(MLA_HEADS), hm(MLA_HEADS // 2)] + tail_specs
    else:
        out_shape = [jax.ShapeDtypeStruct((g, r, MLA_HEADS * HEAD_LANES), F32)] + tail_shapes
        out_specs = [tok(MLA_HEADS * HEAD_LANES)] + tail_specs
    return pl.pallas_call(
        functools.partial(_ab_in_kernel, prompt=prompt, gb=gb, rb=rb),
        out_shape=out_shape, grid=grid, in_specs=in_specs, out_specs=out_specs,
        compiler_params=_params("arbitrary", "arbitrary"),
        name="ab_in_prompt" if prompt else "ab_in_sample",
    )(*ins)


def _triangle(nq, descending):
    qi, kj = [], []
    for q in range(nq):
        ks = range(q, -1, -1) if descending else range(q + 1)
        for k in ks:
            qi.append(q)
            kj.append(k)
    return jnp.asarray(np.array(qi, np.int32)), jnp.asarray(np.array(kj, np.int32))


def _mla_prompt_kernel(qi_ref, kj_ref, q_ref, k_ref, v_ref, o_ref, m_scr, l_scr, acc_scr, *, tq):
    t = pl.program_id(1)
    qi, kj = qi_ref[t], kj_ref[t]

    @pl.when(kj == 0)
    def _init():
        m_scr[...] = jnp.full(m_scr.shape, NEG_INF, F32)
        l_scr[...] = jnp.zeros(l_scr.shape, F32)
        acc_scr[...] = jnp.zeros(acc_scr.shape, F32)

    def run(diagonal):
        if diagonal:
            row = lax.broadcasted_iota(jnp.int32, (tq, tq), 0)
            col = lax.broadcasted_iota(jnp.int32, (tq, tq), 1)
            valid = col <= row

        def body(i, carry):
            hds = [4 * i + n for n in range(4)]
            ss = [_dot_nt(q_ref[0, hd], k_ref[0, hd]) * MLA_SCALE for hd in hds]
            if diagonal:
                ss = [jnp.where(valid, s, NEG_INF) for s in ss]
            ps, alphas = [], []
            for hd, s in zip(hds, ss):
                m_prev = m_scr[hd]
                m_new = jnp.maximum(m_prev, jnp.max(s, axis=-1, keepdims=True))
                alpha = jnp.exp(m_prev - m_new)
                p = jnp.exp(s - _lanes(m_new, tq))
                l_scr[hd] = alpha * l_scr[hd] + jnp.sum(p, axis=-1, keepdims=True)
                m_scr[hd] = m_new
                ps.append(p.astype(BF16))
                alphas.append(alpha)
            for n, (hd, p, alpha) in enumerate(zip(hds, ps, alphas)):
                acc_scr[hd] = alpha * acc_scr[hd] + _dot(p, v_ref[0, 2 * i + n // 2])
            return carry

        lax.fori_loop(0, MLA_HEADS // 4, body, 0)

    @pl.when(kj < qi)
    def _off():
        run(False)

    @pl.when(kj == qi)
    def _diag():
        run(True)
        lane = lax.broadcasted_iota(jnp.int32, (tq, LANES), 1)
        for hp in range(MLA_HEADS // 2):
            o0 = acc_scr[2 * hp] / l_scr[2 * hp]
            o1 = acc_scr[2 * hp + 1] / l_scr[2 * hp + 1]
            o_ref[0, :, hp * LANES:(hp + 1) * LANES] = jnp.where(lane < V_HEAD, o0, o1).astype(BF16)


def _mla_prompt(q, k, v, tq):
    b, nh, s, _ = q.shape
    nq = s // tq
    qi, kj = _triangle(nq, descending=False)
    grid_spec = pltpu.PrefetchScalarGridSpec(
        num_scalar_prefetch=2, grid=(b, qi.shape[0]),
        in_specs=[
            pl.BlockSpec((1, nh, tq, LANES), lambda bb, t, qi_r, kj_r: (bb, 0, qi_r[t], 0)),
            pl.BlockSpec((1, nh, tq, LANES), lambda bb, t, qi_r, kj_r: (bb, 0, kj_r[t], 0)),
            pl.BlockSpec((1, nh // 2, tq, LANES), lambda bb, t, qi_r, kj_r: (bb, 0, kj_r[t], 0)),
        ],
        out_specs=pl.BlockSpec((1, tq, nh * V_HEAD), lambda bb, t, qi_r, kj_r: (bb, qi_r[t], 0)),
        scratch_shapes=[pltpu.VMEM((nh, tq, LANES), F32), pltpu.VMEM((nh, tq, LANES), F32),
                        pltpu.VMEM((nh, tq, LANES), F32)],
    )
    return pl.pallas_call(
        functools.partial(_mla_prompt_kernel, tq=tq),
        out_shape=jax.ShapeDtypeStruct((b, s, nh * V_HEAD), BF16),
        grid_spec=grid_spec,
        compiler_params=_params("arbitrary", "arbitrary"),
        name="mla_prompt",
    )(qi, kj, q, k, v)


def _mla_sample_kernel(pt_ref, q_ref, cn_ref, kn_ref, wuk_ref, wuv_ref, *rest, pages, rows):
    c_refs, k_refs = rest[:pages], rest[pages:2 * pages]
    o_ref, qabs_scr, qpe_scr, m_scr, l_scr, acc_scr = rest[2 * pages:]
    j = pl.program_id(1)
    nrow = MLA_HEADS * rows

    def softmax_step(state, scores, values):
        m_prev, l_prev, acc_prev = state
        smax = scores[0]
        for s in scores[1:]:
            smax = jnp.maximum(smax, s)
        m_new = jnp.maximum(m_prev, jnp.max(smax, axis=-1, keepdims=True))
        alpha = jnp.exp(m_prev - m_new)
        ps = [jnp.exp(s - _lanes(m_new, s.shape[1])) for s in scores]
        psum = ps[0]
        for p in ps[1:]:
            psum = psum + p
        pv = None
        for p, c in zip(ps, values):
            d = _dot(p.astype(BF16), c)
            pv = d if pv is None else pv + d
        return (m_new, alpha * l_prev + jnp.sum(psum, axis=-1, keepdims=True),
                _lanes(alpha, KV_LORA) * acc_prev + pv)

    def load_state():
        return m_scr[...], l_scr[...], acc_scr[...]

    def store_state(state):
        m_scr[...], l_scr[...], acc_scr[...] = state

    @pl.when(j == 0)
    def _first():
        q = q_ref[0]
        q64 = jnp.concatenate([q[:, hd * HEAD_LANES:(hd + 1) * HEAD_LANES] for hd in range(MLA_HEADS)],
                              axis=0)
        q64_bf = q64.astype(BF16)
        rgrp = _idiv(lax.broadcasted_iota(jnp.int32, (nrow, KV_LORA), 0), rows)
        qabs = jnp.zeros((nrow, KV_LORA), F32)
        for hd in range(MLA_HEADS):
            qabs = qabs + jnp.where(rgrp == hd, _dot(q64_bf, wuk_ref[hd]), 0.0)
        qabs_scr[...] = qabs.astype(BF16)
        qpe_scr[...] = q64_bf[:, :QK_ROPE]
        m_scr[...] = jnp.full(m_scr.shape, NEG_INF, F32)
        l_scr[...] = jnp.zeros(l_scr.shape, F32)
        acc_scr[...] = jnp.zeros(acc_scr.shape, F32)
        pad = PAGE_SIZE - rows
        cn = jnp.concatenate([cn_ref[0], jnp.zeros((pad, KV_LORA), F32)], axis=0).astype(BF16)
        kn = jnp.concatenate([kn_ref[0], jnp.zeros((pad, QK_ROPE), F32)], axis=0).astype(BF16)
        row = lax.broadcasted_iota(jnp.int32, (nrow, PAGE_SIZE), 0)
        col = lax.broadcasted_iota(jnp.int32, (nrow, PAGE_SIZE), 1)
        s = (_dot_nt(qabs_scr[...], cn) + _dot_nt(qpe_scr[...], kn)) * MLA_SCALE
        store_state(softmax_step(load_state(), [jnp.where(col <= _imod(row, rows), s, NEG_INF)], [cn]))

    qabs, qpe = qabs_scr[...], qpe_scr[...]

    def group_scores(first_page):
        idx = range(first_page, min(first_page + MLA_PAGES_PER_GROUP, pages), 2)
        cs = [jnp.concatenate([c_refs[i][0, 0], c_refs[i + 1][0, 0]], axis=0).astype(BF16) for i in idx]
        ks = [jnp.concatenate([k_refs[i][0, 0], k_refs[i + 1][0, 0]], axis=1).astype(BF16) for i in idx]
        return [(_dot_nt(qabs, c) + _dot(qpe, k)) * MLA_SCALE for c, k in zip(cs, ks)], cs

    firsts = list(range(0, pages, MLA_PAGES_PER_GROUP))
    state = load_state()
    nxt = group_scores(firsts[0])
    for gi in range(len(firsts)):
        cur = nxt
        if gi + 1 < len(firsts):
            nxt = group_scores(firsts[gi + 1])
        state = softmax_step(state, *cur)
    store_state(state)

    @pl.when(j == pl.num_programs(1) - 1)
    def _last():
        o_lat = (acc_scr[...] / _lanes(l_scr[...], KV_LORA)).astype(BF16)
        o_full = _dot(o_lat, wuv_ref[...])
        lane_h = _idiv(lax.broadcasted_iota(jnp.int32, (rows, MLA_HEADS * V_HEAD), 1), V_HEAD)
        out = jnp.zeros((rows, MLA_HEADS * V_HEAD), F32)
        for hd in range(MLA_HEADS):
            out = out + jnp.where(lane_h == hd, o_full[hd * rows:(hd + 1) * rows], 0.0)
        o_ref[0] = out


def _mla_sample(q, ckv_new, kpe_new, cache_ckv, cache_kpe, e, page_table, wukp, wuv, pages):
    b, rows, _ = q.shape
    n_pages = page_table.shape[1]
    assert n_pages % pages == 0 and pages % 2 == 0 and rows == SUBLANES
    steps = n_pages // pages
    pt = page_table.reshape(-1)
    nrow = MLA_HEADS * rows

    def page_spec(shape, i):
        return pl.BlockSpec((1, 1) + shape,
                            lambda bb, j, pt_r, _i=i: (e, pt_r[bb * n_pages + j * pages + _i], 0, 0))

    cache_kpe_t = jnp.transpose(cache_kpe, (0, 1, 3, 2))
    full = lambda a: pl.BlockSpec(a.shape, lambda bb, j, pt_r, _n=a.ndim: (0,) * _n)
    seq = lambda c: pl.BlockSpec((1, rows, c), lambda bb, j, pt_r: (bb, 0, 0))
    grid_spec = pltpu.PrefetchScalarGridSpec(
        num_scalar_prefetch=1, grid=(b, steps),
        in_specs=[seq(q.shape[-1]), seq(KV_LORA), seq(QK_ROPE), full(wukp), full(wuv)]
        + [page_spec((PAGE_SIZE, KV_LORA), i) for i in range(pages)]
        + [page_spec((QK_ROPE, PAGE_SIZE), i) for i in range(pages)],
        out_specs=seq(MLA_HEADS * V_HEAD),
        scratch_shapes=[pltpu.VMEM((nrow, KV_LORA), BF16), pltpu.VMEM((nrow, QK_ROPE), BF16),
                        pltpu.VMEM((nrow, LANES), F32), pltpu.VMEM((nrow, LANES), F32),
                        pltpu.VMEM((nrow, KV_LORA), F32)],
    )
    return pl.pallas_call(
        functools.partial(_mla_sample_kernel, pages=pages, rows=rows),
        out_shape=jax.ShapeDtypeStruct((b, rows, MLA_HEADS * V_HEAD), F32),
        grid_spec=grid_spec,
        compiler_params=_params("arbitrary", "arbitrary"),
        name="mla_sample",
    )(pt, q, ckv_new, kpe_new, wukp, wuv, *([cache_ckv] * pages), *([cache_kpe_t] * pages))


def _gelu_tanh(x):
    return x * (0.5 * (1.0 + jnp.tanh(0.7978845608028654 * (x + 0.044715 * (x * x * x)))))


def _lru_kernel(u_ref, gate_ref, cbuf_ref, h0_ref, cw_ref, cb_ref, wa_ref, ba_ref, wx_ref, bx_ref, lam_ref,
                out_ref, hlast_ref, cnew_ref, ext_scr, a_scr, b_scr, h_scr, hc_scr, *, gb, rb):
    j = pl.program_id(1)
    tm, w = gb * rb, LRU_WIDTH
    pre = SUBLANES
    tail = CONV_W - 1

    @pl.when(j == 0)
    def _first():
        ext_scr[:, 0:pre, :] = jnp.zeros((gb, pre, w), F32)
        ext_scr[:, pre - tail:pre, :] = cbuf_ref[...]
        hc_scr[...] = h0_ref[...]

    ext_scr[:, pre:pre + rb, :] = u_ref[...]
    uc = cb_ref[...][None]
    for t in range(CONV_W):
        uc = uc + ext_scr[:, pre - tail + t:pre - tail + t + rb, :] * cw_ref[t:t + 1, :][None]
    new_tail = ext_scr[:, pre + rb - tail:pre + rb, :]
    ext_scr[:, pre - tail:pre, :] = new_tail
    cnew_ref[...] = new_tail

    uc = uc.reshape(tm, w)
    ub = uc.astype(BF16)
    r = _sigmoid(_dot(ub, wa_ref[...]) + ba_ref[...])
    i = _sigmoid(_dot(ub, wx_ref[...]) + bx_ref[...])
    nl = -lam_ref[...]
    softplus = jnp.maximum(nl, 0.0) + jnp.log(1.0 + jnp.exp(-jnp.abs(nl)))
    log_a = (-LRU_C) * r * softplus
    a = jnp.exp(log_a)
    bt = jnp.sqrt(1.0 - jnp.exp(2.0 * log_a)) * (i * uc)

    rin = _imod(lax.broadcasted_iota(jnp.int32, (tm, w), 0), SUBLANES)
    for sft in (1, 2, 4):
        a_sh = pltpu.roll(a, sft, 0)
        b_sh = pltpu.roll(bt, sft, 0)
        m = rin >= sft
        bt = jnp.where(m, a * b_sh + bt, bt)
        a = jnp.where(m, a * a_sh, a)

    if rb == SUBLANES:
        hseq = a.reshape(gb, rb, w) * hc_scr[...] + bt.reshape(gb, rb, w)
        hc_scr[...] = hseq[:, rb - 1:rb, :]
    else:
        a_scr[...] = a
        b_scr[...] = bt

        def body(g, hprev):
            r0 = pl.multiple_of(g * SUBLANES, SUBLANES)
            hh = a_scr[pl.ds(r0, SUBLANES), :] * hprev + b_scr[pl.ds(r0, SUBLANES), :]
            h_scr[pl.ds(r0, SUBLANES), :] = hh
            return hh[SUBLANES - 1:SUBLANES, :]

        hc_scr[0] = lax.fori_loop(0, rb // SUBLANES, body, hc_scr[0])
        hseq = h_scr[...].reshape(gb, rb, w)
    hlast_ref[...] = hc_scr[...]
    out_ref[...] = (hseq * _gelu_tanh(gate_ref[...])).astype(out_ref.dtype)


def _lru(u, gate, conv_buf, h0, wts, tm_target):
    g, r, w = u.shape
    gb, rb = _token_tiles(g, r, tm_target)
    assert gb == 1 or rb == SUBLANES
    tm = gb * rb
    full = lambda a: pl.BlockSpec(a.shape, lambda i, j, _n=a.ndim: (0,) * _n)
    tok = pl.BlockSpec((gb, rb, w), lambda i, j: (i, j, 0))
    per_g = lambda rows: pl.BlockSpec((gb, rows, w), lambda i, j: (i, 0, 0))
    names = ["conv_w", "conv_b", "wa", "ba", "wx", "bx", "lam"]
    return pl.pallas_call(
        functools.partial(_lru_kernel, gb=gb, rb=rb),
        out_shape=[jax.ShapeDtypeStruct((g, r, w), _act_dtype(gb)), jax.ShapeDtypeStruct((g, 1, w), F32),
                   jax.ShapeDtypeStruct((g, CONV_W - 1, w), F32)],
        grid=(g // gb, r // rb),
        in_specs=[tok, tok, per_g(CONV_W - 1), per_g(1)] + [full(wts[n]) for n in names],
        out_specs=[tok, per_g(1), per_g(CONV_W - 1)],
        scratch_shapes=[pltpu.VMEM((gb, SUBLANES + rb, w), F32), pltpu.VMEM((tm, w), F32),
                        pltpu.VMEM((tm, w), F32), pltpu.VMEM((tm, w), F32), pltpu.VMEM((gb, 1, w), F32)],
        compiler_params=_params("arbitrary", "arbitrary"),
        name="lru",
    )(u, gate, conv_buf, h0, *[wts[n] for n in names])


def _mix_out_kernel(*refs, n_in, gb, rb, alpha):
    a_refs, w_refs = refs[:n_in], refs[n_in:2 * n_in]
    x_ref, gt_ref, lng_ref, lnb_ref, sc2_ref, sh2_ref, wr_ref, br_ref, x1_ref, h2_ref, lg_ref = refs[2 * n_in:]
    tm = gb * rb
    mix = None
    for a_ref, w_ref in zip(a_refs, w_refs):
        part = _dot(a_ref[...].reshape(tm, a_ref.shape[-1]).astype(BF16), w_ref[...])
        mix = part if mix is None else mix + part
    y = alpha * x_ref[...] + (1.0 + gt_ref[...]) * mix.reshape(gb, rb, D_MODEL)
    x1 = _layer_norm(y, lng_ref[...], lnb_ref[...])
    x1_ref[...] = x1
    h2 = x1 * (1.0 + sc2_ref[...]) + sh2_ref[...]
    h2_ref[...] = h2.astype(h2_ref.dtype)
    lg_ref[...] = _dot_nt(wr_ref[...], h2.reshape(tm, D_MODEL).astype(BF16)) + br_ref[...]


def _mix_out(acts, ws, x, gt, lng, lnb, sc2, sh2, wr, br, alpha, tm_target):
    g, r, d = x.shape
    gb, rb = _token_tiles(g, r, tm_target)
    tm = gb * rb
    nj = r // rb
    full = lambda a: pl.BlockSpec(a.shape, lambda i, j, _n=a.ndim: (0,) * _n)
    tok = lambda c: pl.BlockSpec((gb, rb, c), lambda i, j: (i, j, 0))
    mod = pl.BlockSpec((gb, 1, d), lambda i, j: (i, 0, 0))
    return pl.pallas_call(
        functools.partial(_mix_out_kernel, n_in=len(acts), gb=gb, rb=rb, alpha=alpha),
        out_shape=[jax.ShapeDtypeStruct((g, r, d), F32), jax.ShapeDtypeStruct((g, r, d), _act_dtype(gb)),
                   jax.ShapeDtypeStruct((ROUTER_ROWS, g * r), F32)],
        grid=(g // gb, nj),
        in_specs=[tok(a.shape[-1]) for a in acts] + [full(w) for w in ws]
        + [tok(d), mod, full(lng), full(lnb), mod, mod, full(wr), full(br)],
        out_specs=[tok(d), tok(d), pl.BlockSpec((ROUTER_ROWS, tm), lambda i, j: (0, i * nj + j))],
        compiler_params=_params("arbitrary", "arbitrary"),
        name="mix_out",
    )(*acts, *ws, x, gt, lng, lnb, sc2, sh2, wr, br)


def _route_kernel(lg_ref, comb_ref):
    tn = lg_ref.shape[1]
    big = 3.0e38
    grow = lax.broadcasted_iota(jnp.int32, (SUBLANES, tn), 0)
    gl = jnp.where(grow < N_GROUPS, lg_ref[0:SUBLANES, :], -big)
    gmax = jnp.max(gl, axis=0, keepdims=True)
    g_idx = jnp.min(jnp.where(gl == gmax, grow, N_GROUPS), axis=0, keepdims=True)
    g_w = 1.0 / jnp.sum(jnp.where(grow < N_GROUPS, jnp.exp(gl - gmax), 0.0), axis=0, keepdims=True)

    el = lg_ref[EXPERT_ROW0:EXPERT_ROW0 + N_EXPERTS, :]
    erow = lax.broadcasted_iota(jnp.int32, (N_EXPERTS, tn), 0)
    ingrp = _idiv(erow, EXP_PER_GROUP) == g_idx
    emax = jnp.max(jnp.where(ingrp, el, -big), axis=0, keepdims=True)
    ex = jnp.where(ingrp, jnp.exp(el - emax), 0.0)
    p = ex / jnp.sum(ex, axis=0, keepdims=True)
    ps = jnp.where(ingrp, p, -1.0)
    p1 = jnp.max(ps, axis=0, keepdims=True)
    i1 = jnp.min(jnp.where(ps == p1, erow, N_EXPERTS), axis=0, keepdims=True)
    ps2 = jnp.where(erow == i1, -1.0, ps)
    p2 = jnp.max(ps2, axis=0, keepdims=True)
    i2 = jnp.min(jnp.where(ps2 == p2, erow, N_EXPERTS), axis=0, keepdims=True)
    tot = p1 + p2
    comb_ref[...] = (jnp.where(erow == i1, g_w * p1 / tot, 0.0)
                     + jnp.where(erow == i2, g_w * p2 / tot, 0.0))


def _route(logits_t):
    n = logits_t.shape[1]
    tn = min(n, 2048)
    return pl.pallas_call(
        _route_kernel,
        out_shape=jax.ShapeDtypeStruct((N_EXPERTS, n), F32),
        grid=(n // tn,),
        in_specs=[pl.BlockSpec((ROUTER_ROWS, tn), lambda i: (0, i))],
        out_specs=pl.BlockSpec((N_EXPERTS, tn), lambda i: (0, i)),
        compiler_params=_params("arbitrary"),
        name="route",
    )(logits_t)


def _moe_kernel(h_ref, comb_ref, w1_ref, w3_ref, w2_ref, x1_ref, gt_ref, lng_ref, lnb_ref, out_ref, acc_scr,
                *, gb, rb, alpha):
    e = pl.program_id(2)
    tm = gb * rb

    @pl.when(e == 0)
    def _init():
        acc_scr[...] = jnp.zeros(acc_scr.shape, F32)

    h = h_ref[...].reshape(tm, D_MODEL).astype(BF16)
    comb = comb_ref[...].reshape(tm, N_EXPERTS)
    lane = lax.broadcasted_iota(jnp.int32, (tm, N_EXPERTS), 1)
    nx = w1_ref.shape[0]
    ces = [jnp.sum(jnp.where(lane == e * nx + x, comb, 0.0), axis=1, keepdims=True) for x in range(nx)]
    rc = min(tm, MOE_ROW_CHUNK)
    for r0 in range(0, tm, rc):
        c = slice(r0, r0 + rc)
        ab = [(_dot(h[c], w1_ref[x]), _dot(h[c], w3_ref[x])) for x in range(nx)]
        hid = [((a * _sigmoid(a)) * b * ces[x][c]).astype(BF16) for x, (a, b) in enumerate(ab)]
        y = None
        for x in range(nx):
            d = _dot(hid[x], w2_ref[x])
            y = d if y is None else y + d
        acc_scr[c, :] += y

    @pl.when(e == pl.num_programs(2) - 1)
    def _fin():
        y = alpha * x1_ref[...] + (1.0 + gt_ref[...]) * acc_scr[...].reshape(gb, rb, D_MODEL)
        out_ref[...] = _layer_norm(y, lng_ref[...], lnb_ref[...])


def _moe(h2, comb, layer, w1, w3, w2, x1, gt, lng, lnb, alpha, tm_target):
    g, r, d = x1.shape
    gb, rb = _token_tiles(g, r, tm_target)
    nx = MOE_EXPERTS_PER_STEP
    assert w1.shape[1] % nx == 0
    tok = lambda c: pl.BlockSpec((gb, rb, c), lambda i, j, e: (i, j, 0))
    full = lambda a: pl.BlockSpec(a.shape, lambda i, j, e, _n=a.ndim: (0,) * _n)
    wspec = lambda a: pl.BlockSpec((None, nx) + a.shape[2:], lambda i, j, e: (layer, e, 0, 0))
    return pl.pallas_call(
        functools.partial(_moe_kernel, gb=gb, rb=rb, alpha=alpha),
        out_shape=jax.ShapeDtypeStruct((g, r, d), F32),
        grid=(g // gb, r // rb, w1.shape[1] // nx),
        in_specs=[tok(d), tok(N_EXPERTS), wspec(w1), wspec(w3), wspec(w2), tok(d),
                  pl.BlockSpec((gb, 1, d), lambda i, j, e: (i, 0, 0)), full(lng), full(lnb)],
        out_specs=tok(d),
        scratch_shapes=[pltpu.VMEM((gb * rb, d), F32)],
        compiler_params=_params("arbitrary", "arbitrary", "arbitrary"),
        name="moe",
    )(h2, comb, w1, w3, w2, x1, gt, lng, lnb)


def _sb_qkv_kernel(x_ref, sc_ref, sh_ref, w_ref, *outs, prompt, gb, rb):
    tm = gb * rb
    h = (x_ref[...] * (1.0 + sc_ref[...]) + sh_ref[...]).reshape(tm, D_MODEL).astype(BF16)
    qkv = _dot(h, w_ref[...])
    q = qkv[:, :D_MODEL] * SB_SCALE
    k = qkv[:, D_MODEL:2 * D_MODEL]
    v = qkv[:, 2 * D_MODEL:]
    if prompt:
        k_ref, v_ref, qh_ref, kh_ref, vh_ref = outs
        for hp in range(SB_HEADS // 2):
            sl = slice(hp * LANES, (hp + 1) * LANES)
            qh_ref[0, hp] = q[:, sl].astype(BF16)
            kh_ref[0, hp] = k[:, sl].astype(BF16)
            vh_ref[0, hp] = v[:, sl].astype(BF16)
    else:
        k_ref, v_ref, q_ref = outs
        q_ref[...] = q.reshape(gb, rb, D_MODEL)
    k_ref[...] = k.reshape(gb, rb, D_MODEL)
    v_ref[...] = v.reshape(gb, rb, D_MODEL)


def _sb_qkv(x, sc, sh, w, prompt, tm_target):
    g, r, d = x.shape
    gb, rb = _token_tiles(g, r, tm_target)
    tok = pl.BlockSpec((gb, rb, d), lambda i, j: (i, j, 0))
    mod = pl.BlockSpec((gb, 1, d), lambda i, j: (i, 0, 0))
    out_shape = [jax.ShapeDtypeStruct((g, r, d), F32)] * 2
    out_specs = [tok, tok]
    if prompt:
        assert gb == 1
        npair = SB_HEADS // 2
        out_shape += [jax.ShapeDtypeStruct((g, npair, r, LANES), BF16)] * 3
        out_specs += [pl.BlockSpec((1, npair, rb, LANES), lambda i, j: (i, 0, j, 0))] * 3
    else:
        out_shape += [jax.ShapeDtypeStruct((g, r, d), F32)]
        out_specs += [tok]
    return pl.pallas_call(
        functools.partial(_sb_qkv_kernel, prompt=prompt, gb=gb, rb=rb),
        out_shape=out_shape, grid=(g // gb, r // rb),
        in_specs=[tok, mod, mod, pl.BlockSpec(w.shape, lambda i, j: (0, 0))],
        out_specs=out_specs,
        compiler_params=_params("arbitrary", "arbitrary"),
        name="sb_qkv_prompt" if prompt else "sb_qkv_sample",
    )(x, sc, sh, w)


def _sb_terms(z, upper, valid):
    lq = jnp.minimum(z, 0.0) - jnp.log(1.0 + jnp.exp(-jnp.abs(z)))
    lk = lq - z
    if valid is not None:
        lk = jnp.where(valid, lk, 0.0)
    hi = lk.astype(BF16)
    lo = (lk - hi.astype(F32)).astype(BF16)
    return lq + (_dot(hi, upper) + _dot(lo, upper)), jnp.sum(lk, axis=-1, keepdims=True)


def _sb_weights(arg, carry, valid):
    w = jnp.exp(arg + _lanes(carry, arg.shape[1]))
    if valid is not None:
        w = jnp.where(valid, w, 0.0)
    return w.astype(BF16)


def _sb_prompt_kernel(qi_ref, kj_ref, q_ref, k_ref, v_ref, up_ref, o_ref, carry_scr, acc_scr, *, tq):
    t = pl.program_id(1)
    qi, kj = qi_ref[t], kj_ref[t]
    npair = SB_HEADS // 2

    @pl.when(kj == qi)
    def _init():
        carry_scr[...] = jnp.zeros(carry_scr.shape, F32)
        acc_scr[...] = jnp.zeros(acc_scr.shape, F32)

    kb = up_ref.shape[0]

    def run(diagonal):
        valid = None
        if diagonal:
            row = lax.broadcasted_iota(jnp.int32, (tq, tq), 0)
            col = lax.broadcasted_iota(jnp.int32, (tq, tq), 1)
            valid = col < row
        lane = lax.broadcasted_iota(jnp.int32, (tq, LANES), 1)
        upper = up_ref[...]

        hsels = [lane < SB_HEAD_DIM, lane >= SB_HEAD_DIM]
        k0s = list(range(tq - kb, -1, -kb))
        r0s = [k0 if diagonal else 0 for k0 in k0s]
        vlds = [None if valid is None else valid[r0:, k0:k0 + kb] for k0, r0 in zip(k0s, r0s)]

        def pad_rows(x, r0):
            return x if r0 == 0 else jnp.concatenate([jnp.zeros((r0, x.shape[1]), x.dtype), x], axis=0)

        def body(i, c):
            hps = [i * SB_PAIRS_PER_TRIP + n for n in range(SB_PAIRS_PER_TRIP)]
            zs = []
            for hp in hps:
                qq, kk = q_ref[0, hp].astype(F32), k_ref[0, hp]
                zs.append([_dot_nt(jnp.where(hsel, qq, 0.0).astype(BF16), kk) for hsel in hsels])
            terms = [[[_sb_terms(z[r0:, k0:k0 + kb], upper, vld) for k0, r0, vld in zip(k0s, r0s, vlds)]
                      for z in zp] for zp in zs]
            for hp, tp in zip(hps, terms):
                vv = v_ref[0, hp]
                out = None
                for sub in range(2):
                    carry = carry_scr[hp, sub]
                    pv = None
                    for (arg, total), k0, r0, vld in zip(tp[sub], k0s, r0s, vlds):
                        d = pad_rows(_dot(_sb_weights(arg, carry[r0:], vld), vv[k0:k0 + kb]), r0)
                        pv = d if pv is None else pv + d
                        carry = carry + pad_rows(total, r0)
                    carry_scr[hp, sub] = carry
                    out = jnp.where(hsels[sub], pv, 0.0 if out is None else out)
                acc_scr[hp] += out
            return c

        lax.fori_loop(0, npair // SB_PAIRS_PER_TRIP, body, 0)

    @pl.when(kj < qi)
    def _off():
        run(False)

    @pl.when(kj == qi)
    def _diag():
        run(True)

    @pl.when(kj == 0)
    def _fin():
        for hp in range(npair):
            o_ref[0, :, hp * LANES:(hp + 1) * LANES] = acc_scr[hp].astype(BF16)


def _upper(n):
    j = np.arange(n)[:, None]
    s = np.arange(n)[None, :]
    return jnp.asarray((j > s).astype(np.float32), dtype=BF16)


def _sb_prompt(q, k, v, tq):
    b, npair, s, _ = q.shape
    nq = s // tq
    qi, kj = _triangle(nq, descending=True)
    kb = SB_CUMSUM_KEYS if tq % SB_CUMSUM_KEYS == 0 else tq
    up = _upper(kb)
    blk = lambda sel: pl.BlockSpec((1, npair, tq, LANES), sel)
    grid_spec = pltpu.PrefetchScalarGridSpec(
        num_scalar_prefetch=2, grid=(b, qi.shape[0]),
        in_specs=[blk(lambda bb, t, qi_r, kj_r: (bb, 0, qi_r[t], 0)),
                  blk(lambda bb, t, qi_r, kj_r: (bb, 0, kj_r[t], 0)),
                  blk(lambda bb, t, qi_r, kj_r: (bb, 0, kj_r[t], 0)),
                  pl.BlockSpec((kb, kb), lambda bb, t, qi_r, kj_r: (0, 0))],
        out_specs=pl.BlockSpec((1, tq, npair * LANES), lambda bb, t, qi_r, kj_r: (bb, qi_r[t], 0)),
        scratch_shapes=[pltpu.VMEM((npair, 2, tq, LANES), F32), pltpu.VMEM((npair, tq, LANES), F32)],
    )
    return pl.pallas_call(
        functools.partial(_sb_prompt_kernel, tq=tq),
        out_shape=jax.ShapeDtypeStruct((b, s, npair * LANES), BF16),
        grid_spec=grid_spec,
        compiler_params=_params("arbitrary", "arbitrary"),
        name="sb_prompt",
    )(qi, kj, q, k, v, up)


def _sb_sample_kernel(pt_ref, q_ref, kn_ref, vn_ref, up_ref, *rest, pages, rows):
    k_refs, v_refs = rest[:pages], rest[pages:2 * pages]
    o_ref, qbd_scr, carry_scr, acc_scr = rest[2 * pages:]
    j = pl.program_id(1)
    nrow = SB_HEADS * rows
    upper = up_ref[...]

    @pl.when(j == 0)
    def _first():
        qt = jnp.concatenate([q_ref[0]] * SB_HEADS, axis=0)
        rgrp = _idiv(lax.broadcasted_iota(jnp.int32, (nrow, D_MODEL), 0), rows)
        lgrp = _idiv(lax.broadcasted_iota(jnp.int32, (nrow, D_MODEL), 1), SB_HEAD_DIM)
        qbd = jnp.where(rgrp == lgrp, qt, 0.0).astype(BF16)
        qbd_scr[...] = qbd
        pad = jnp.zeros((PAGE_SIZE - rows, D_MODEL), F32)
        kn = jnp.concatenate([kn_ref[0], pad], axis=0).astype(BF16)
        vn = jnp.concatenate([vn_ref[0], pad], axis=0).astype(BF16)
        row = lax.broadcasted_iota(jnp.int32, (nrow, PAGE_SIZE), 0)
        col = lax.broadcasted_iota(jnp.int32, (nrow, PAGE_SIZE), 1)
        valid = col < _imod(row, rows)
        arg, total = _sb_terms(_dot_nt(qbd, kn), upper[:PAGE_SIZE, :PAGE_SIZE], valid)
        w = _sb_weights(arg, jnp.zeros((nrow, LANES), F32), valid)
        carry_scr[...] = jnp.broadcast_to(total, (nrow, LANES))
        acc_scr[...] = _dot(w, vn)

    qbd = qbd_scr[...]
    pairs = range(0, pages, 2)
    zs = [_dot(qbd, jnp.concatenate([k_refs[i + 1][0, 0], k_refs[i][0, 0]], axis=1).astype(BF16))
          for i in pairs]
    terms = [_sb_terms(z, upper, None) for z in zs]
    carry = carry_scr[...]
    pv = None
    for i, (arg, total) in zip(pairs, terms):
        vt = jnp.concatenate([v_refs[i + 1][0, 0], v_refs[i][0, 0]], axis=1).astype(BF16)
        d = _dot_nt(_sb_weights(arg, carry, None), vt)
        pv = d if pv is None else pv + d
        carry = carry + total
    carry_scr[...] = carry
    acc_scr[...] += pv

    @pl.when(j == pl.num_programs(1) - 1)
    def _last():
        lgrp = _idiv(lax.broadcasted_iota(jnp.int32, (rows, D_MODEL), 1), SB_HEAD_DIM)
        out = jnp.zeros((rows, D_MODEL), F32)
        for hd in range(SB_HEADS):
            out = out + jnp.where(lgrp == hd, acc_scr[hd * rows:(hd + 1) * rows, :], 0.0)
        o_ref[0] = out


def _sb_sample(q, k_new, v_new, cache_k, cache_v, o_idx, page_table, pages):
    b, rows, d = q.shape
    n_pages = page_table.shape[1]
    assert n_pages % pages == 0 and pages % 2 == 0 and rows == SUBLANES
    steps = n_pages // pages
    pt = page_table.reshape(-1)
    nrow = SB_HEADS * rows
    up = _upper(2 * PAGE_SIZE)

    def page_spec(i):
        return pl.BlockSpec(
            (1, 1, d, PAGE_SIZE),
            lambda bb, j, pt_r, _i=i: (o_idx, pt_r[bb * n_pages + n_pages - 1 - (j * pages + _i)], 0, 0))

    seq = pl.BlockSpec((1, rows, d), lambda bb, j, pt_r: (bb, 0, 0))
    grid_spec = pltpu.PrefetchScalarGridSpec(
        num_scalar_prefetch=1, grid=(b, steps),
        in_specs=[seq, seq, seq, pl.BlockSpec(up.shape, lambda bb, j, pt_r: (0, 0))]
        + [page_spec(i) for i in range(pages)] * 2,
        out_specs=seq,
        scratch_shapes=[pltpu.VMEM((nrow, d), BF16), pltpu.VMEM((nrow, LANES), F32), pltpu.VMEM((nrow, d), F32)],
    )
    return pl.pallas_call(
        functools.partial(_sb_sample_kernel, pages=pages, rows=rows),
        out_shape=jax.ShapeDtypeStruct((b, rows, d), F32),
        grid_spec=grid_spec,
        compiler_params=_params("arbitrary", "arbitrary"),
        name="sb_sample",
    )(pt, q, k_new, v_new, up, *([cache_k] * pages), *([cache_v] * pages))


def _prep_layer0(e, w_in_ab, g_q, w_uq, g_kv, w_uk, w_uv, conv_w, conv_b, w_rg_a, b_rg_a, w_rg_x, b_rg_x,
                 lru_lambda):
    half = QK_ROPE // 2
    w_in = w_in_ab[e]
    o1, o2, o3, o4 = Q_LORA, Q_LORA + KV_LORA, Q_LORA + KV_LORA + QK_ROPE, Q_LORA + KV_LORA + QK_ROPE + LRU_WIDTH
    kpe_w = w_in[:, o2:o3]
    zpad = jnp.zeros((D_MODEL, LANES - QK_ROPE), F32)
    win = jnp.concatenate([
        w_in[:, :o2], w_in[:, o3:o4], w_in[:, o4:],
        kpe_w, zpad,
        kpe_w[:, half:], kpe_w[:, :half], zpad], axis=1).astype(BF16)

    wq = w_uq[e].reshape(Q_LORA, MLA_HEADS, QK_NOPE + QK_ROPE)
    nope, x1, x2 = wq[..., :QK_NOPE], wq[..., QK_NOPE:QK_NOPE + half], wq[..., QK_NOPE + half:]
    z32 = jnp.zeros((Q_LORA, MLA_HEADS, HEAD_LANES - QK_NOPE - QK_ROPE), F32)
    wqa = jnp.concatenate([x1, x2, z32, nope], axis=-1).reshape(Q_LORA, MLA_HEADS * HEAD_LANES).astype(BF16)
    wqb = jnp.concatenate([x2, x1, z32, jnp.zeros_like(nope)], axis=-1)
    wqb = wqb.reshape(Q_LORA, MLA_HEADS * HEAD_LANES).astype(BF16)

    wk_top = jnp.concatenate([jnp.zeros((KV_LORA, MLA_HEADS, HEAD_LANES - QK_NOPE), F32), w_uk[e]], axis=-1)
    eye = jnp.eye(LANES, HEAD_LANES, dtype=F32) * (jnp.arange(LANES) < QK_ROPE)[:, None]
    wk_bot = jnp.broadcast_to(eye[:, None, :], (LANES, MLA_HEADS, HEAD_LANES))
    wkp = jnp.concatenate([wk_top, wk_bot], axis=0).reshape(KV_LORA + LANES, MLA_HEADS * HEAD_LANES).astype(BF16)
    wv = w_uv[e].reshape(KV_LORA, MLA_HEADS * V_HEAD).astype(BF16)
    wukp = jnp.concatenate([jnp.zeros((MLA_HEADS, HEAD_LANES - QK_NOPE, KV_LORA), F32),
                            jnp.transpose(w_uk[e], (1, 2, 0))], axis=1).astype(BF16)

    def block_diag(wb):
        n, k, _ = wb.shape
        eye_n = jnp.eye(n, dtype=F32)
        return (wb[:, :, None, :] * eye_n[:, None, :, None]).reshape(n * k, n * k).astype(BF16)

    row = lambda v: v.reshape(1, -1)
    return dict(
        win=win, gq=row(g_q[e]), gkv=row(g_kv[e]), wqa=wqa, wqb=wqb, wkp=wkp, wv=wv, wukp=wukp,
        conv_w=conv_w[e], conv_b=row(conv_b[e]), wa=block_diag(w_rg_a[e]), ba=row(b_rg_a[e]),
        wx=block_diag(w_rg_x[e]), bx=row(b_rg_x[e]), lam=row(lru_lambda[e]))


def _rope_tables(pos):
    half = QK_ROPE // 2
    inv = ROPE_THETA ** (-jnp.arange(half, dtype=F32) / half)
    ang = pos.astype(F32)[:, None] * inv
    cos, sin = jnp.cos(ang), jnp.sin(ang)
    n = pos.shape[0]
    z32 = jnp.zeros((n, HEAD_LANES - QK_NOPE - QK_ROPE), F32)
    ctab = jnp.concatenate([cos, cos, z32, jnp.ones((n, QK_NOPE), F32)], axis=1)
    stab = jnp.concatenate([-sin, sin, z32, jnp.zeros((n, QK_NOPE), F32)], axis=1)
    return ctab, stab


def _router_weights(w_rg, b_rg, w_re, b_re):
    wr = jnp.zeros((ROUTER_ROWS, D_MODEL), F32)
    wr = wr.at[:N_GROUPS].set(w_rg.T).at[EXPERT_ROW0:EXPERT_ROW0 + N_EXPERTS].set(w_re.T)
    br = jnp.zeros((ROUTER_ROWS, 1), F32)
    br = br.at[:N_GROUPS, 0].set(b_rg).at[EXPERT_ROW0:EXPERT_ROW0 + N_EXPERTS, 0].set(b_re)
    return wr.astype(BF16), br


def _run_group(x, c, pos, past, prep, tiles):
    depth = len(prep["layers"])
    alpha = (2.0 * depth) ** 0.25
    prompt = past is None
    g, r, d = x.shape
    mods = _adaln(c, prep["w_mod"], prep["b_mod"])
    ctab, stab = _rope_tables(pos)
    ckv_l, kpe_l, h_l, conv_l, k_l, v_l = [], [], [], [], [], []
    for l in range(depth):
        sh1, sc1, gt1, sh2, sc2, gt2 = mods[l]
        lw = prep["layers"][l]
        if l % 2 == 0:
            e = l // 2
            w0 = lw["mixer"]
            if prompt:
                q, kf, vv, ckv, kpe, u, gate = _ab_in(x, sc1, sh1, w0, ctab, stab, True, tiles["tok"])
                attn = _mla_prompt(q, kf, vv, tiles["attn"])
                h0 = jnp.zeros((g, 1, LRU_WIDTH), F32)
                conv_buf = jnp.zeros((g, CONV_W - 1, LRU_WIDTH), F32)
            else:
                cache_ckv, cache_kpe, st_h, st_conv, _, _, page_table = past
                q, ckv, kpe, u, gate = _ab_in(x, sc1, sh1, w0, ctab, stab, False, tiles["tok"])
                attn = _mla_sample(q, ckv, kpe, cache_ckv, cache_kpe, e, page_table, w0["wukp"], w0["wv"],
                                   tiles["mla_pages"])
                h0 = st_h[e][:, None, :]
                conv_buf = st_conv[e]
            lru_out, h_last, conv_new = _lru(u, gate, conv_buf, h0, w0, tiles["lru"])
            acts, ws = [attn, lru_out], [lw["w_out_a"], lw["w_out_b"]]
            ckv_l.append(ckv)
            kpe_l.append(kpe)
            h_l.append(h_last[:, 0, :])
            conv_l.append(conv_new)
        else:
            o = l // 2
            if prompt:
                k, v, qh, kh, vh = _sb_qkv(x, sc1, sh1, lw["w_qkv"], True, tiles["tok"])
                att = _sb_prompt(qh, kh, vh, tiles["attn"])
            else:
                k, v, q = _sb_qkv(x, sc1, sh1, lw["w_qkv"], False, tiles["tok"])
                att = _sb_sample(q, k, v, past[4], past[5], o, past[6], tiles["sb_pages"])
            acts, ws = [att], [lw["w_out"]]
            k_l.append(k.reshape(g, r, SB_HEADS, SB_HEAD_DIM))
            v_l.append(v.reshape(g, r, SB_HEADS, SB_HEAD_DIM))
        x1, h2, logits_t = _mix_out(acts, ws, x, gt1, lw["ln_g1"], lw["ln_b1"], sc2, sh2, lw["wr"], lw["br"],
                                    alpha, tiles["tok"])
        comb = _route(logits_t).T.reshape(g, r, N_EXPERTS)
        x = _moe(h2, comb, l, prep["w1"], prep["w3"], prep["w2"], x1, gt2, lw["ln_g2"], lw["ln_b2"], alpha,
                 tiles["moe"])
    return x, (jnp.stack(ckv_l), jnp.stack(kpe_l), jnp.stack(h_l), jnp.stack(conv_l), jnp.stack(k_l),
               jnp.stack(v_l))


def kernel(x_prompt, x_sample, cache_mla_ckv, cache_mla_kpe, cache_sb_k, cache_sb_v, state_lru_h, state_conv,
           page_table, c_prompt, c_sample, w_mod, b_mod, ln_g, ln_b, w_in_ab, g_q, w_uq, g_kv, w_uk, w_uv,
           conv_w, conv_b, w_rg_a, b_rg_a, w_rg_x, b_rg_x, lru_lambda, w_out_ab, w_qkv_c, w_out_c,
           w_router_g, b_router_g, w_router_e, b_router_e, w_e1, w_e3, w_e2):
    depth = w_mod.shape[0]
    row = lambda v: v.reshape(1, -1)
    layers = []
    for l in range(depth):
        wr, br = _router_weights(w_router_g[l], b_router_g[l], w_router_e[l], b_router_e[l])
        lw = dict(ln_g1=row(ln_g[l, 0]), ln_b1=row(ln_b[l, 0]), ln_g2=row(ln_g[l, 1]), ln_b2=row(ln_b[l, 1]),
                  wr=wr, br=br)
        if l % 2 == 0:
            e = l // 2
            lw["mixer"] = _prep_layer0(e, w_in_ab, g_q, w_uq, g_kv, w_uk, w_uv, conv_w, conv_b, w_rg_a, b_rg_a,
                                       w_rg_x, b_rg_x, lru_lambda)
            wo = w_out_ab[e].astype(BF16)
            lw["w_out_a"], lw["w_out_b"] = wo[:MLA_HEADS * V_HEAD], wo[MLA_HEADS * V_HEAD:]
        else:
            o = l // 2
            lw["w_qkv"] = w_qkv_c[o].astype(BF16)
            lw["w_out"] = w_out_c[o].astype(BF16)
        layers.append(lw)
    prep = dict(w_mod=w_mod, b_mod=b_mod, layers=layers,
                w1=w_e1.astype(BF16), w3=w_e3.astype(BF16), w2=w_e2.astype(BF16))

    n_pool = cache_sb_k.shape[1]
    cache_k = jnp.transpose(cache_sb_k, (0, 1, 3, 4, 2)).reshape(cache_sb_k.shape[0], n_pool, D_MODEL, PAGE_SIZE)
    cache_v = jnp.transpose(cache_sb_v, (0, 1, 3, 4, 2)).reshape(cache_sb_v.shape[0], n_pool, D_MODEL, PAGE_SIZE)
    n_pages = page_table.shape[1]
    past = (cache_mla_ckv, cache_mla_kpe, state_lru_h, state_conv, cache_k, cache_v, page_table)

    seq = x_prompt.shape[1]
    tiles_p = dict(tok=min(256, seq), attn=min(512, seq), lru=min(512, seq), moe=min(1024, seq))
    n_s = x_sample.shape[0] * x_sample.shape[1]
    tiles_s = dict(tok=min(256, n_s), lru=n_s, moe=min(1024, n_s),
                   mla_pages=min(64, n_pages), sb_pages=min(8, n_pages))

    pos_p = jnp.arange(seq, dtype=jnp.int32)
    pos_s = n_pages * PAGE_SIZE + jnp.arange(x_sample.shape[1], dtype=jnp.int32)
    y_p, (ckv_p, kpe_p, h_p, conv_p, k_p, v_p) = _run_group(x_prompt, c_prompt, pos_p, None, prep, tiles_p)
    y_s, (ckv_s, kpe_s, h_s, conv_s, k_s, v_s) = _run_group(x_sample, c_sample, pos_s, past, prep, tiles_s)
    return (y_p, y_s, ckv_p, kpe_p, h_p, conv_p, k_p, v_p, ckv_s, kpe_s, h_s, conv_s, k_s, v_s)
```
